```python
import math
import jax, jax.numpy as jnp
from jax import lax
import numpy as np

D_MODEL = 1024
BATCH = 8
SEQ = 2048
DEPTH = 2
DEC_BATCH = 32
DEC_SEQ = 8
PAST_LEN = 16384
PAGE_SIZE = 128

N_META = 16
Q_BLOCK = 128
SB_HEADS = 8
SB_DIM = 64
SB_WIDTH = SB_HEADS * SB_DIM
DF_HEADS = 4
DF_DIM = 64
DF_QK_WIDTH = DF_HEADS * 2 * DF_DIM
DF_V_WIDTH = DF_HEADS * 2 * DF_DIM
SPLIT_SIZES = (SB_WIDTH, SB_WIDTH, SB_WIDTH, DF_QK_WIDTH, DF_QK_WIDTH, DF_V_WIDTH, D_MODEL, D_MODEL)
IN_WIDTH = sum(SPLIT_SIZES)
ROPE_THETA = 10000.0
ALPHA = (2 * DEPTH) ** 0.25
BETA = (8 * DEPTH) ** -0.25
LN_EPS = 1e-5
HEAD_NORM_EPS = 1e-5
PEER_HEADS = 8
PEER_NKEYS = 128
PEER_EXPERTS = PEER_NKEYS * PEER_NKEYS
PEER_TOPK = 16
PEER_QDIM = 256
PEER_HALF = PEER_QDIM // 2
PEER_CHUNK = 256

kernel_name = 'stickbreak_diffattn_peer_hybrid_step'


def split_points():
    pts, acc = [], 0
    for s in SPLIT_SIZES[:-1]:
        acc += s
        pts.append(acc)
    return pts


def layer_norm(x, g, b):
    xf = x.astype(jnp.float32)
    mu = jnp.mean(xf, axis=-1, keepdims=True)
    var = jnp.mean(jnp.square(xf - mu), axis=-1, keepdims=True)
    y = (xf - mu) * lax.rsqrt(var + LN_EPS) * g.astype(jnp.float32) + b.astype(jnp.float32)
    return y.astype(x.dtype)


def rope(x, pos):
    half = x.shape[-1] // 2
    inv = ROPE_THETA ** (-jnp.arange(half, dtype=jnp.float32) / half)
    ang = pos[:, None] * inv[None, :]
    cos = jnp.cos(ang)[None, :, None, None, :]
    sin = jnp.sin(ang)[None, :, None, None, :]
    xf = x.astype(jnp.float32)
    x1, x2 = xf[..., :half], xf[..., half:]
    return jnp.concatenate([x1 * cos - x2 * sin, x2 * cos + x1 * sin], axis=-1).astype(x.dtype)


def stick_breaking_attention(q, k, v, q_off):
    tq = q.shape[1]
    scale = q.shape[-1] ** -0.5
    outs = []
    for b0 in range(0, tq, Q_BLOCK):
        b1 = min(b0 + Q_BLOCK, tq)
        kend = q_off + b1
        z = jnp.einsum('bqhd,bkhd->bhqk', q[:, b0:b1], k[:, :kend]).astype(jnp.float32) * scale
        qpos = q_off + jnp.arange(b0, b1)
        kpos = jnp.arange(kend)
        mask = kpos[None, :] < qpos[:, None]
        log_keep = jnp.where(mask, jax.nn.log_sigmoid(-z), 0.0)
        between = lax.cumsum(log_keep, axis=3, reverse=True) - log_keep
        a = jnp.where(mask, jnp.exp(jax.nn.log_sigmoid(z) + between), 0.0)
        outs.append(jnp.einsum('bhqk,bkhd->bqhd', a.astype(v.dtype), v[:, :kend]))
    return jnp.concatenate(outs, axis=1)


def differential_attention(q, k, v, lam, q_off):
    tq = q.shape[1]
    scale = q.shape[-1] ** -0.5
    neg = jnp.finfo(jnp.float32).min
    outs = []
    for b0 in range(0, tq, Q_BLOCK):
        b1 = min(b0 + Q_BLOCK, tq)
        kend = q_off + b1
        s = jnp.einsum('bqhcd,bkhcd->bhcqk', q[:, b0:b1], k[:, :kend]).astype(jnp.float32) * scale
        qpos = q_off + jnp.arange(b0, b1)
        kpos = jnp.arange(kend)
        mask = kpos[None, :] <= qpos[:, None]
        p = jax.nn.softmax(jnp.where(mask, s, neg), axis=-1)
        w = p[:, :, 0] - lam * p[:, :, 1]
        outs.append(jnp.einsum('bhqk,bkhe->bqhe', w.astype(v.dtype), v[:, :kend]))
    return jnp.concatenate(outs, axis=1)


def peer_ffn(x, w_pq, sub_keys, u_tab, v_tab):
    n = x.shape[0]
    q = (x @ w_pq).reshape(n, PEER_HEADS, 2, PEER_HALF)
    s = jnp.einsum('nhcd,hckd->nhck', q, sub_keys).astype(jnp.float32)
    s1, i1 = lax.top_k(s[:, :, 0], PEER_TOPK)
    s2, i2 = lax.top_k(s[:, :, 1], PEER_TOPK)
    cand_s = (s1[..., :, None] + s2[..., None, :]).reshape(n, PEER_HEADS, PEER_TOPK * PEER_TOPK)
    cand_i = (i1[..., :, None] * PEER_NKEYS + i2[..., None, :]).reshape(n, PEER_HEADS, PEER_TOPK * PEER_TOPK)
    best, sel = lax.top_k(cand_s, PEER_TOPK)
    eidx = jnp.take_along_axis(cand_i, sel, axis=-1)
    g = jax.nn.softmax(best, axis=-1)
    pad = (-n) % PEER_CHUNK
    xp = jnp.pad(x, ((0, pad), (0, 0))).reshape(-1, PEER_CHUNK, x.shape[1])
    ep = jnp.pad(eidx, ((0, pad), (0, 0), (0, 0))).reshape(-1, PEER_CHUNK, PEER_HEADS, PEER_TOPK)
    gp = jnp.pad(g, ((0, pad), (0, 0), (0, 0))).reshape(-1, PEER_CHUNK, PEER_HEADS, PEER_TOPK)

    def chunk(args):
        xc, ec, gc = args
        a = jnp.einsum('cd,chkd->chk', xc, u_tab[ec]).astype(jnp.float32)
        a = jax.nn.gelu(a, approximate=False) * gc
        return jnp.einsum('chk,chkd->cd', a.astype(xc.dtype), v_tab[ec])

    y = lax.map(chunk, (xp, ep, gp)).reshape(-1, x.shape[1])
    return y[:n]


def layer_forward(x, pos, past, layer, w_in, b_gate, w_br_a, w_br_b, w_o, lam_q1, lam_k1, lam_q2, lam_k2,
                  df_norm_g, ln1_g, ln1_b, peer_wq, peer_sub_keys, peer_u, peer_v, ln2_g, ln2_b):
    f32 = jnp.float32
    B, T, _ = x.shape
    proj = jnp.einsum('btd,de->bte', x, w_in)
    q_sb, k_sb, v_sb, q_df, k_df, v_df, g_a, g_b = jnp.split(proj, split_points(), axis=-1)
    q_sb = q_sb.reshape(B, T, SB_HEADS, SB_DIM)
    k_sb = k_sb.reshape(B, T, SB_HEADS, SB_DIM)
    v_sb = v_sb.reshape(B, T, SB_HEADS, SB_DIM)
    q_df = rope(q_df.reshape(B, T, DF_HEADS, 2, DF_DIM), pos)
    k_df = rope(k_df.reshape(B, T, DF_HEADS, 2, DF_DIM), pos)
    v_df = v_df.reshape(B, T, DF_HEADS, 2 * DF_DIM)
    new_rows = (k_sb, v_sb, k_df, v_df)
    if past is None:
        kk_sb, vv_sb, kk_df, vv_df = new_rows
    else:
        kk_sb, vv_sb, kk_df, vv_df = [jnp.concatenate([p.astype(r.dtype), r], axis=1) for p, r in zip(past, new_rows)]
    q_off = kk_sb.shape[1] - T
    o_a = stick_breaking_attention(q_sb, kk_sb, vv_sb, q_off).reshape(B, T, SB_WIDTH)
    lam_init = 0.8 - 0.6 * math.exp(-0.3 * layer)
    lam = (jnp.exp(jnp.sum(lam_q1.astype(f32) * lam_k1.astype(f32)))
           - jnp.exp(jnp.sum(lam_q2.astype(f32) * lam_k2.astype(f32))) + lam_init)
    o_b = differential_attention(q_df, kk_df, vv_df, lam, q_off).astype(f32)
    o_b = o_b * lax.rsqrt(jnp.mean(jnp.square(o_b), axis=-1, keepdims=True) + HEAD_NORM_EPS)
    o_b = (o_b * df_norm_g.astype(f32) * (1.0 - lam_init)).reshape(B, T, DF_V_WIDTH).astype(x.dtype)
    gate_a = jax.nn.sigmoid(g_a + b_gate[:D_MODEL])
    gate_b = jax.nn.sigmoid(g_b + b_gate[D_MODEL:])
    merged = gate_a * (o_a @ w_br_a) + gate_b * (o_b @ w_br_b)
    h = layer_norm(ALPHA * x + merged @ w_o, ln1_g, ln1_b)
    f = peer_ffn(h.reshape(B * T, D_MODEL), peer_wq, peer_sub_keys, peer_u, peer_v).reshape(B, T, D_MODEL)
    out = layer_norm(ALPHA * h + f, ln2_g, ln2_b)
    return out, new_rows


def gather_pages(cache_l, page_table):
    g = cache_l[page_table]
    return g.reshape((page_table.shape[0], page_table.shape[1] * cache_l.shape[1]) + cache_l.shape[2:])


def setup_inputs(seed: int = 0) -> dict:
    key = jax.random.key(seed)
    ks = jax.random.split(key, 32)
    f32 = jnp.float32
    n_pages = PAST_LEN // PAGE_SIZE
    n_used = DEC_BATCH * n_pages
    n_pool = n_used + max(1, n_used // 4)

    def nrm(k, shape, s=1.0):
        return jax.random.normal(k, shape, f32) * s

    col_scale = np.concatenate([
        np.ones(2 * SB_WIDTH), np.full(SB_WIDTH, BETA), np.ones(2 * DF_QK_WIDTH),
        np.full(DF_V_WIDTH, BETA), np.ones(2 * D_MODEL)]).astype(np.float32)
    x_prompt = nrm(ks[0], (BATCH, SEQ, D_MODEL))
    x_sample = nrm(ks[1], (DEC_BATCH, DEC_SEQ, D_MODEL))
    cache_sb_k = nrm(ks[2], (DEPTH, n_pool, PAGE_SIZE, SB_HEADS, SB_DIM))
    cache_sb_v = nrm(ks[3], (DEPTH, n_pool, PAGE_SIZE, SB_HEADS, SB_DIM), BETA)
    cache_df_k = nrm(ks[4], (DEPTH, n_pool, PAGE_SIZE, DF_HEADS, 2, DF_DIM))
    cache_df_v = nrm(ks[5], (DEPTH, n_pool, PAGE_SIZE, DF_HEADS, 2 * DF_DIM), BETA)
    page_table = jax.random.permutation(ks[6], n_pool)[:n_used].reshape(DEC_BATCH, n_pages).astype(jnp.int32)
    meta_tokens = nrm(ks[7], (N_META, D_MODEL))
    w_in = nrm(ks[8], (DEPTH, D_MODEL, IN_WIDTH), D_MODEL ** -0.5) * jnp.asarray(col_scale)
    b_gate = nrm(ks[9], (DEPTH, 2 * D_MODEL), 0.1)
    w_br_a = nrm(ks[10], (DEPTH, SB_WIDTH, D_MODEL), SB_WIDTH ** -0.5)
    w_br_b = nrm(ks[11], (DEPTH, DF_V_WIDTH, D_MODEL), DF_V_WIDTH ** -0.5)
    w_o = nrm(ks[12], (DEPTH, D_MODEL, D_MODEL), BETA * D_MODEL ** -0.5)
    lam_q1 = nrm(ks[13], (DEPTH, DF_DIM), 0.1)
    lam_k1 = nrm(ks[14], (DEPTH, DF_DIM), 0.1)
    lam_q2 = nrm(ks[15], (DEPTH, DF_DIM), 0.1)
    lam_k2 = nrm(ks[16], (DEPTH, DF_DIM), 0.1)
    df_norm_g = 1.0 + nrm(ks[17], (DEPTH, DF_HEADS, 2 * DF_DIM), 0.02)
    ln1_g = 1.0 + nrm(ks[18], (DEPTH, D_MODEL), 0.02)
    ln1_b = nrm(ks[19], (DEPTH, D_MODEL), 0.02)
    peer_wq = nrm(ks[20], (DEPTH, D_MODEL, PEER_HEADS * PEER_QDIM), D_MODEL ** -0.5)
    peer_sub_keys = nrm(ks[21], (DEPTH, PEER_HEADS, 2, PEER_NKEYS, PEER_HALF), PEER_HALF ** -0.5)
    peer_u = nrm(ks[22], (DEPTH, PEER_EXPERTS, D_MODEL), D_MODEL ** -0.5)
    peer_v = nrm(ks[23], (DEPTH, PEER_EXPERTS, D_MODEL), BETA)
    ln2_g = 1.0 + nrm(ks[24], (DEPTH, D_MODEL), 0.02)
    ln2_b = nrm(ks[25], (DEPTH, D_MODEL), 0.02)
    return {
        'x_prompt': x_prompt, 'x_sample': x_sample,
        'cache_sb_k': cache_sb_k, 'cache_sb_v': cache_sb_v,
        'cache_df_k': cache_df_k, 'cache_df_v': cache_df_v,
        'page_table': page_table, 'meta_tokens': meta_tokens,
        'w_in': w_in, 'b_gate': b_gate, 'w_br_a': w_br_a, 'w_br_b': w_br_b, 'w_o': w_o,
        'lam_q1': lam_q1, 'lam_k1': lam_k1, 'lam_q2': lam_q2, 'lam_k2': lam_k2,
        'df_norm_g': df_norm_g, 'ln1_g': ln1_g, 'ln1_b': ln1_b,
        'peer_wq': peer_wq, 'peer_sub_keys': peer_sub_keys, 'peer_u': peer_u, 'peer_v': peer_v,
        'ln2_g': ln2_g, 'ln2_b': ln2_b,
    }


def reference(x_prompt, x_sample, cache_sb_k, cache_sb_v, cache_df_k, cache_df_v, page_table, meta_tokens,
              w_in, b_gate, w_br_a, w_br_b, w_o, lam_q1, lam_k1, lam_q2, lam_k2, df_norm_g, ln1_g, ln1_b,
              peer_wq, peer_sub_keys, peer_u, peer_v, ln2_g, ln2_b):
    f32 = jnp.float32

    def params(l):
        return (w_in[l], b_gate[l], w_br_a[l], w_br_b[l], w_o[l], lam_q1[l], lam_k1[l], lam_q2[l], lam_k2[l],
                df_norm_g[l], ln1_g[l], ln1_b[l], peer_wq[l], peer_sub_keys[l], peer_u[l], peer_v[l],
                ln2_g[l], ln2_b[l])

    b = x_prompt.shape[0]
    meta = jnp.broadcast_to(meta_tokens[None].astype(x_prompt.dtype), (b, N_META, D_MODEL))
    hp = jnp.concatenate([meta, x_prompt], axis=1)
    pos_p = jnp.arange(hp.shape[1], dtype=f32)
    rows_p = []
    for l in range(DEPTH):
        hp, new = layer_forward(hp, pos_p, None, l, *params(l))
        rows_p.append(new)
    y_prompt = hp[:, N_META:]

    past_len = page_table.shape[1] * cache_sb_k.shape[2]
    pos_s = past_len + jnp.arange(x_sample.shape[1], dtype=f32)
    hs = x_sample
    rows_s = []
    for l in range(DEPTH):
        past = (gather_pages(cache_sb_k[l], page_table), gather_pages(cache_sb_v[l], page_table),
                gather_pages(cache_df_k[l], page_table), gather_pages(cache_df_v[l], page_table))
        hs, new = layer_forward(hs, pos_s, past, l, *params(l))
        rows_s.append(new)
    y_sample = hs

    new_sb_k_prompt = jnp.stack([r[0] for r in rows_p])
    new_sb_v_prompt = jnp.stack([r[1] for r in rows_p])
    new_df_k_prompt = jnp.stack([r[2] for r in rows_p])
    new_df_v_prompt = jnp.stack([r[3] for r in rows_p])
    new_sb_k_sample = jnp.stack([r[0] for r in rows_s])
    new_sb_v_sample = jnp.stack([r[1] for r in rows_s])
    new_df_k_sample = jnp.stack([r[2] for r in rows_s])
    new_df_v_sample = jnp.stack([r[3] for r in rows_s])
    return (y_prompt, y_sample, new_sb_k_prompt, new_sb_v_prompt, new_df_k_prompt, new_df_v_prompt,
            new_sb_k_sample, new_sb_v_sample, new_df_k_sample, new_df_v_sample)
```

```python
import functools
import math

import jax
import jax.numpy as jnp
import numpy as np
from jax import lax
from jax.experimental import pallas as pl
from jax.experimental.pallas import tpu as pltpu

F32 = jnp.float32
BF16 = jnp.bfloat16

LANES = 128
SUBLANES = 8
VMEM_LIMIT = 56 * 1024 * 1024

N_META = 16
Q_BLOCK = 128
SB_HEADS = 8
SB_DIM = 64
DF_HEADS = 4
DF_DIM = 64
ROPE_THETA = 10000.0
LN_EPS = 1e-5
HEAD_NORM_EPS = 1e-5
PEER_HEADS = 8
PEER_NKEYS = 128
PEER_TOPK = 16
PEER_HALF = 128
NEG = float(np.finfo(np.float32).min)

STAIR = tuple((r1, PEER_TOPK // (r1 + 1)) for r1 in range(PEER_TOPK))
STAIR_ROWS = sum(n for _, n in STAIR)
STAIR_PAD = -(-STAIR_ROWS // SUBLANES) * SUBLANES


def _dot(a, b):
    return jnp.dot(a, b, preferred_element_type=F32)


def _dot_nt(a, b):
    return lax.dot_general(a, b, (((1,), (1,)), ((), ())), preferred_element_type=F32)


def _layer_norm(x, g, b):
    mu = jnp.mean(x, axis=-1, keepdims=True)
    xc = x - mu
    var = jnp.mean(xc * xc, axis=-1, keepdims=True)
    return xc * lax.rsqrt(var + LN_EPS) * g + b


def _params(sem, vmem=VMEM_LIMIT):
    return pltpu.CompilerParams(dimension_semantics=sem, vmem_limit_bytes=vmem)


def _rope(p, cos, sin):
    lane = lax.broadcasted_iota(jnp.int32, (p.shape[0], LANES), 1)
    first_half = (lane % DF_DIM) < (DF_DIM // 2)
    outs = []
    for s in range(p.shape[1] // LANES):
        slab = p[:, s * LANES:(s + 1) * LANES]
        rot = jnp.where(first_half, pltpu.roll(slab, LANES - DF_DIM // 2, 1),
                        pltpu.roll(slab, DF_DIM // 2, 1))
        outs.append(slab * cos + rot * sin)
    return jnp.concatenate(outs, axis=1)


def _inproj_kernel(x_ref, w_ref, bg_ref, cos_ref, sin_ref,
                   qsb_ref, ksb_ref, vsb_ref, qdf_ref, kdf_ref, vdf_ref,
                   ksb16_ref, vsb16_ref, kdf16_ref, vdf16_ref, ga_ref, gb_ref):
    x = x_ref[...]
    cos = cos_ref[...]
    sin = sin_ref[...]
    w = 512

    def proj(c0, c1):
        return _dot(x, w_ref[:, c0:c1])

    qsb_ref[...] = (proj(0, w) * (SB_DIM ** -0.5)).astype(BF16)
    k = proj(w, 2 * w)
    ksb_ref[...] = k
    ksb16_ref[...] = k.astype(BF16)
    v = proj(2 * w, 3 * w)
    vsb_ref[...] = v
    vsb16_ref[...] = v.astype(BF16)
    qdf_ref[...] = (_rope(proj(3 * w, 4 * w), cos, sin) * (DF_DIM ** -0.5)).astype(BF16)
    k = _rope(proj(4 * w, 5 * w), cos, sin)
    kdf_ref[...] = k
    kdf16_ref[...] = k.astype(BF16)
    v = proj(5 * w, 6 * w)
    vdf_ref[...] = v
    vdf16_ref[...] = v.astype(BF16)
    d = ga_ref.shape[1]
    ga_ref[...] = jax.nn.sigmoid(proj(6 * w, 6 * w + d) + bg_ref[:, :d]).astype(BF16)
    gb_ref[...] = jax.nn.sigmoid(proj(6 * w + d, 6 * w + 2 * d) + bg_ref[:, d:]).astype(BF16)


def _inproj(x16, w16, bg, cos, sin, tm):
    n, d = x16.shape
    nblk = cos.shape[0] // tm
    row = lambda i: (i, 0)
    const = lambda i: (0, 0)
    tab = lambda i: (i % nblk, 0)
    wide = 512
    out_shape = ([jax.ShapeDtypeStruct((n, wide), BF16)] + [jax.ShapeDtypeStruct((n, wide), F32)] * 2
                 + [jax.ShapeDtypeStruct((n, wide), BF16)] + [jax.ShapeDtypeStruct((n, wide), F32)] * 2
                 + [jax.ShapeDtypeStruct((n, wide), BF16)] * 4 + [jax.ShapeDtypeStruct((n, d), BF16)] * 2)
    out_specs = [pl.BlockSpec((tm, wide), row)] * 10 + [pl.BlockSpec((tm, d), row)] * 2
    return pl.pallas_call(
        _inproj_kernel,
        grid=(n // tm,),
        in_specs=[pl.BlockSpec((tm, d), row),
                  pl.BlockSpec(w16.shape, const),
                  pl.BlockSpec(bg.shape, const),
                  pl.BlockSpec((tm, LANES), tab),
                  pl.BlockSpec((tm, LANES), tab)],
        out_specs=out_specs,
        out_shape=out_shape,
        compiler_params=_params(("parallel",)),
        name="inproj",
    )(x16, w16, bg, cos, sin)


def _strict_lower(n):
    r = lax.broadcasted_iota(jnp.int32, (n, n), 0)
    c = lax.broadcasted_iota(jnp.int32, (n, n), 1)
    return (r > c).astype(BF16)


def _sb_block(qm, kb, vb, lmat, mask, csum, acc):
    z = _dot_nt(qm, kb)
    ls = jnp.minimum(z, 0.0) - jnp.log1p(jnp.exp(-jnp.abs(z)))
    lk = ls - z
    if mask is not None:
        lk = jnp.where(mask, lk, 0.0)
    hi = lk.astype(BF16)
    lo = (lk - hi.astype(F32)).astype(BF16)
    between = csum + _dot(hi, lmat) + _dot(lo, lmat)
    a = jnp.exp(ls + between)
    if mask is not None:
        a = jnp.where(mask, a, 0.0)
    acc = acc + _dot(a.astype(BF16), vb)
    csum = between[:, :1] + lk[:, :1]
    return csum, acc


def _sb_prefill_kernel(q_ref, k_ref, v_ref, o_ref, *, tq):
    qi = pl.program_id(2)
    q = q_ref[0].astype(F32)
    lane = lax.broadcasted_iota(jnp.int32, (tq, LANES), 1)
    row = lax.broadcasted_iota(jnp.int32, (tq, tq), 0)
    col = lax.broadcasted_iota(jnp.int32, (tq, tq), 1)
    lmat = _strict_lower(tq)
    outs = []
    for sub in range(2):
        in_head = (lane >= SB_DIM) if sub else (lane < SB_DIM)
        qm = jnp.where(in_head, q, 0.0).astype(BF16)

        def body(j, carry, qm=qm):
            csum, acc = carry
            start = pl.multiple_of((qi - j) * tq, tq)
            kb = k_ref[0, pl.ds(start, tq), :]
            vb = v_ref[0, pl.ds(start, tq), :]
            mask = col < row + j * tq
            return _sb_block(qm, kb, vb, lmat, mask, csum, acc)

        _, acc = lax.fori_loop(0, qi + 1, body,
                               (jnp.zeros((tq, 1), F32), jnp.zeros((tq, LANES), F32)))
        outs.append(acc)
    o_ref[0] = jnp.where(lane < SB_DIM, outs[0], outs[1]).astype(o_ref.dtype)


def _sb_prefill(q16, k16, v16, tq=Q_BLOCK):
    b, t, w = q16.shape
    blk_q = pl.BlockSpec((1, tq, LANES), lambda bi, hp, qi: (bi, qi, hp))
    blk_kv = pl.BlockSpec((1, t, LANES), lambda bi, hp, qi: (bi, 0, hp))
    return pl.pallas_call(
        functools.partial(_sb_prefill_kernel, tq=tq),
        grid=(b, w // LANES, t // tq),
        in_specs=[blk_q, blk_kv, blk_kv],
        out_specs=blk_q,
        out_shape=jax.ShapeDtypeStruct((b, t, w), BF16),
        compiler_params=_params(("parallel", "parallel", "arbitrary")),
        name="sb_prefill",
    )(q16, k16, v16)


def _softmax_block(qm, kb, vb, mask, m, l, acc):
    s = _dot_nt(qm, kb)
    if mask is not None:
        s = jnp.where(mask, s, NEG)
    m_new = jnp.maximum(m, jnp.max(s, axis=1, keepdims=True))
    alpha = jnp.exp(m - m_new)
    p = jnp.exp(s - m_new)
    l = alpha * l + jnp.sum(p, axis=1, keepdims=True)
    acc = alpha * acc + _dot(p.astype(BF16), vb)
    return m_new, l, acc


def _head_norm(o, g, out_scale):
    ms = jnp.mean(o * o, axis=-1, keepdims=True)
    return o * lax.rsqrt(ms + HEAD_NORM_EPS) * g * out_scale


def _df_prefill_kernel(lam_ref, q_ref, k_ref, v_ref, g_ref, o_ref, *, tq, out_scale):
    qi = pl.program_id(2)
    q = q_ref[0].astype(F32)
    lane = lax.broadcasted_iota(jnp.int32, (tq, LANES), 1)
    row = lax.broadcasted_iota(jnp.int32, (tq, tq), 0)
    col = lax.broadcasted_iota(jnp.int32, (tq, tq), 1)
    comps = []
    for c in range(2):
        in_comp = (lane >= DF_DIM) if c else (lane < DF_DIM)
        qm = jnp.where(in_comp, q, 0.0).astype(BF16)

        def body(j, carry, qm=qm):
            start = pl.multiple_of(j * tq, tq)
            kb = k_ref[0, pl.ds(start, tq), :]
            vb = v_ref[0, pl.ds(start, tq), :]
            mask = col + j * tq <= row + qi * tq
            return _softmax_block(qm, kb, vb, mask, *carry)

        m, l, acc = lax.fori_loop(
            0, qi + 1, body,
            (jnp.full((tq, 1), NEG, F32), jnp.zeros((tq, 1), F32), jnp.zeros((tq, LANES), F32)))
        comps.append(acc / l)
    o = comps[0] - lam_ref[0] * comps[1]
    o_ref[0] = _head_norm(o, g_ref[0], out_scale).astype(o_ref.dtype)


def _df_prefill(lam, q16, k16, v16, g, out_scale, tq=Q_BLOCK):
    b, t, w = q16.shape
    blk_q = pl.BlockSpec((1, tq, LANES), lambda bi, h, qi: (bi, qi, h))
    blk_kv = pl.BlockSpec((1, t, LANES), lambda bi, h, qi: (bi, 0, h))
    return pl.pallas_call(
        functools.partial(_df_prefill_kernel, tq=tq, out_scale=out_scale),
        grid=(b, w // LANES, t // tq),
        in_specs=[pl.BlockSpec(memory_space=pltpu.SMEM), blk_q, blk_kv, blk_kv,
                  pl.BlockSpec((1, 1, LANES), lambda bi, h, qi: (h, 0, 0))],
        out_specs=blk_q,
        out_shape=jax.ShapeDtypeStruct((b, t, w), BF16),
        compiler_params=_params(("parallel", "parallel", "arbitrary")),
        name="df_prefill",
    )(lam, q16, k16, v16, g.reshape(DF_HEADS, 1, LANES))


def _decode_kernel(pt_ref, lam_ref, qsb_ref, ksbn_ref, vsbn_ref, qdf_ref, kdfn_ref, vdfn_ref, g_ref,
                   *rest, pg, n_new, out_scale):
    del pt_ref
    sbk = rest[0:pg]
    sbv = rest[pg:2 * pg]
    dfk = rest[2 * pg:3 * pg]
    dfv = rest[3 * pg:4 * pg]
    oa_ref, ob_ref = rest[4 * pg:4 * pg + 2]
    qxa_ref, qxb_ref, csum_ref, acca_ref, m_ref, l_ref, accb_ref = rest[4 * pg + 2:]
    c = pl.program_id(1)
    nrow, width = qxa_ref.shape
    page = sbk[0].shape[0]
    lmat = _strict_lower(page)

    @pl.when(c == 0)
    def _():
        lane = lax.broadcasted_iota(jnp.int32, (n_new, width), 1)
        qa = qsb_ref[...].astype(F32)
        qb = qdf_ref[...].astype(F32)
        for h in range(nrow // n_new):
            sel = (lane // (width * n_new // nrow)) == h
            qxa_ref[h * n_new:(h + 1) * n_new, :] = jnp.where(sel, qa, 0.0)
            qxb_ref[h * n_new:(h + 1) * n_new, :] = jnp.where(sel, qb, 0.0)
        pad = jnp.zeros((page - n_new, width), F32)
        tok = lax.broadcasted_iota(jnp.int32, (nrow, page), 0) % n_new
        key = lax.broadcasted_iota(jnp.int32, (nrow, page), 1)
        kb = jnp.concatenate([ksbn_ref[...], pad], axis=0).astype(BF16)
        vb = jnp.concatenate([vsbn_ref[...], pad], axis=0).astype(BF16)
        csum, acc = _sb_block(qxa_ref[...].astype(BF16), kb, vb, lmat, key < tok,
                              jnp.zeros((nrow, 1), F32), jnp.zeros((nrow, width), F32))
        csum_ref[...] = csum
        acca_ref[...] = acc
        kb = jnp.concatenate([kdfn_ref[...], pad], axis=0).astype(BF16)
        vb = jnp.concatenate([vdfn_ref[...], pad], axis=0).astype(BF16)
        m, l, acc = _softmax_block(qxb_ref[...].astype(BF16), kb, vb, key <= tok,
                                   jnp.full((nrow, 1), NEG, F32), jnp.zeros((nrow, 1), F32),
                                   jnp.zeros((nrow, width), F32))
        m_ref[...] = m
        l_ref[...] = l
        accb_ref[...] = acc

    qxa = qxa_ref[...].astype(BF16)
    qxb = qxb_ref[...].astype(BF16)
    csum = csum_ref[...]
    acca = acca_ref[...]
    m, l, accb = m_ref[...], l_ref[...], accb_ref[...]
    for g in range(pg):
        csum, acca = _sb_block(qxa, sbk[g][...].astype(BF16), sbv[g][...].astype(BF16), lmat, None,
                               csum, acca)
        m, l, accb = _softmax_block(qxb, dfk[g][...].astype(BF16), dfv[g][...].astype(BF16), None,
                                    m, l, accb)
    csum_ref[...] = csum
    acca_ref[...] = acca
    m_ref[...] = m
    l_ref[...] = l
    accb_ref[...] = accb

    @pl.when(c == pl.num_programs(1) - 1)
    def _():
        lane = lax.broadcasted_iota(jnp.int32, (n_new, width), 1)
        out = jnp.zeros((n_new, width), F32)
        for h in range(SB_HEADS):
            out = out + jnp.where(lane // SB_DIM == h, acca[h * n_new:(h + 1) * n_new, :], 0.0)
        oa_ref[...] = out.astype(oa_ref.dtype)
        o = accb / l
        for h in range(DF_HEADS):
            lanes = slice(h * LANES, (h + 1) * LANES)
            o0 = o[(2 * h) * n_new:(2 * h + 1) * n_new, lanes]
            o1 = o[(2 * h + 1) * n_new:(2 * h + 2) * n_new, lanes]
            oh = _head_norm(o0 - lam_ref[0] * o1, g_ref[h:h + 1, :], out_scale)
            ob_ref[:, lanes] = oh.astype(ob_ref.dtype)


def _decode_attention(layer, page_table, lam, qsb, ksbn, vsbn, qdf, kdfn, vdfn, g,
                      c_sbk, c_sbv, c_dfk, c_dfv, out_scale, pg=8):
    nb, n_new, width = qsb.shape
    n_pages = page_table.shape[1]
    page = c_sbk.shape[2]
    new_spec = pl.BlockSpec((None, n_new, width), lambda b, c, pt: (b, 0, 0))

    def page_spec(g_):
        def imap(b, c, pt):
            return (layer, pt[b, n_pages - 1 - (c * pg + g_)], 0, 0)
        return pl.BlockSpec((None, None, page, width), imap)

    page_specs = [page_spec(g_) for g_ in range(pg)]
    nrow = SB_HEADS * n_new
    grid_spec = pltpu.PrefetchScalarGridSpec(
        num_scalar_prefetch=1,
        grid=(nb, n_pages // pg),
        in_specs=[pl.BlockSpec(memory_space=pltpu.SMEM)] + [new_spec] * 6
                 + [pl.BlockSpec(g.shape, lambda b, c, pt: (0, 0))] + page_specs * 4,
        out_specs=[new_spec, new_spec],
        scratch_shapes=[pltpu.VMEM((nrow, width), F32), pltpu.VMEM((nrow, width), F32),
                        pltpu.VMEM((nrow, 1), F32), pltpu.VMEM((nrow, width), F32),
                        pltpu.VMEM((nrow, 1), F32), pltpu.VMEM((nrow, 1), F32),
                        pltpu.VMEM((nrow, width), F32)],
    )
    return pl.pallas_call(
        functools.partial(_decode_kernel, pg=pg, n_new=n_new, out_scale=out_scale),
        grid_spec=grid_spec,
        out_shape=[jax.ShapeDtypeStruct((nb, n_new, width), BF16)] * 2,
        compiler_params=_params(("parallel", "arbitrary")),
        name="decode_attention",
    )(page_table, lam, qsb, ksbn, vsbn, qdf, kdfn, vdfn, g,
      *([c_sbk] * pg), *([c_sbv] * pg), *([c_dfk] * pg), *([c_dfv] * pg))


def _merge_kernel(x_ref, oa_ref, ob_ref, ga_ref, gb_ref, wa_ref, wb_ref, wo_ref, g_ref, b_ref,
                  h_ref, h16_ref, *, alpha):
    merged = (ga_ref[...].astype(F32) * _dot(oa_ref[...], wa_ref[...])
              + gb_ref[...].astype(F32) * _dot(ob_ref[...], wb_ref[...]))
    y = alpha * x_ref[...] + _dot(merged.astype(BF16), wo_ref[...])
    h = _layer_norm(y, g_ref[...], b_ref[...])
    h_ref[...] = h
    h16_ref[...] = h.astype(BF16)


def _merge(x, oa, ob, ga, gb, wa, wb, wo, g, b, alpha, tm):
    n, d = x.shape
    row = lambda i: (i, 0)
    const = lambda i: (0, 0)
    return pl.pallas_call(
        functools.partial(_merge_kernel, alpha=alpha),
        grid=(n // tm,),
        in_specs=[pl.BlockSpec((tm, d), row), pl.BlockSpec((tm, oa.shape[1]), row),
                  pl.BlockSpec((tm, ob.shape[1]), row), pl.BlockSpec((tm, d), row),
                  pl.BlockSpec((tm, d), row), pl.BlockSpec(wa.shape, const),
                  pl.BlockSpec(wb.shape, const), pl.BlockSpec(wo.shape, const),
                  pl.BlockSpec(g.shape, const), pl.BlockSpec(b.shape, const)],
        out_specs=[pl.BlockSpec((tm, d), row)] * 2,
        out_shape=[jax.ShapeDtypeStruct((n, d), F32), jax.ShapeDtypeStruct((n, d), BF16)],
        compiler_params=_params(("parallel",)),
        name="merge_ln",
    )(x, oa, ob, ga, gb, wa, wb, wo, g, b)


def _extract_topk(s, k, val_ref, idx_ref, row0):
    n = s.shape[0]
    iota = lax.broadcasted_iota(jnp.int32, s.shape, 0)
    for r in range(k):
        m = jnp.max(s, axis=0, keepdims=True)
        idx = jnp.min(jnp.where(s == m, iota, n), axis=0, keepdims=True)
        val_ref[pl.ds(row0 + r, 1), :] = m
        idx_ref[pl.ds(row0 + r, 1), :] = idx
        s = jnp.where(iota == idx, -jnp.inf, s)


def _route_kernel(h_ref, wq_ref, keys_ref, i1_ref, i2_ref, g_ref,
                  s_ref, v_ref, i_ref, cand_ref, ci_ref, best_ref, pos_ref, sel_ref):
    tm = h_ref.shape[0]
    nsub = 2 * PEER_HEADS
    q = _dot(h_ref[...], wq_ref[...]).astype(BF16)
    for hc in range(nsub):
        s_ref[hc] = _dot_nt(keys_ref[hc], q[:, hc * PEER_HALF:(hc + 1) * PEER_HALF])

    def level1(hc, _):
        _extract_topk(s_ref[hc], PEER_TOPK, v_ref, i_ref, pl.multiple_of(hc * PEER_TOPK, PEER_TOPK))
        return 0

    lax.fori_loop(0, nsub, level1, 0)

    cand_ref[STAIR_ROWS:, :] = jnp.full((STAIR_PAD - STAIR_ROWS, tm), -jnp.inf, F32)
    ci_ref[STAIR_ROWS:, :] = jnp.zeros((STAIR_PAD - STAIR_ROWS, tm), jnp.int32)

    def level2(h, _):
        base1 = pl.multiple_of(2 * h * PEER_TOPK, PEER_TOPK)
        base2 = pl.multiple_of((2 * h + 1) * PEER_TOPK, PEER_TOPK)
        off = 0
        for r1, n2 in STAIR:
            cand_ref[off:off + n2, :] = v_ref[pl.ds(base1 + r1, 1), :] + v_ref[pl.ds(base2, n2), :]
            ci_ref[off:off + n2, :] = (i_ref[pl.ds(base1 + r1, 1), :] * PEER_NKEYS
                                       + i_ref[pl.ds(base2, n2), :])
            off += n2
        _extract_topk(cand_ref[...], PEER_TOPK, best_ref, pos_ref, 0)
        best = best_ref[...]
        e = jnp.exp(best - best[:1, :])
        gate = e / jnp.sum(e, axis=0, keepdims=True)
        ci = ci_ref[...]
        iota = lax.broadcasted_iota(jnp.int32, ci.shape, 0)
        out0 = pl.multiple_of(h * PEER_TOPK, PEER_TOPK)
        for r in range(PEER_TOPK):
            eidx = jnp.max(jnp.where(iota == pos_ref[r:r + 1, :], ci, -1), axis=0, keepdims=True)
            sel_ref[0, pl.ds(out0 + r, 1), :] = (eidx >> 7).astype(F32)
            sel_ref[1, pl.ds(out0 + r, 1), :] = (eidx & (PEER_NKEYS - 1)).astype(F32)
        sel_ref[2, pl.ds(out0, PEER_TOPK), :] = gate
        return 0

    lax.fori_loop(0, PEER_HEADS, level2, 0)
    for c0 in range(0, tm, LANES):
        i1_ref[c0:c0 + LANES, :] = sel_ref[0, :, c0:c0 + LANES].T
        i2_ref[c0:c0 + LANES, :] = sel_ref[1, :, c0:c0 + LANES].T
        g_ref[c0:c0 + LANES, :] = sel_ref[2, :, c0:c0 + LANES].T


def _route(h16, wq16, keys16, tm):
    n, d = h16.shape
    nsub = 2 * PEER_HEADS
    nsel = PEER_HEADS * PEER_TOPK
    row = lambda i: (i, 0)
    return pl.pallas_call(
        _route_kernel,
        grid=(n // tm,),
        in_specs=[pl.BlockSpec((tm, d), row), pl.BlockSpec(wq16.shape, lambda i: (0, 0)),
                  pl.BlockSpec(keys16.shape, lambda i: (0, 0, 0))],
        out_specs=[pl.BlockSpec((tm, nsel), row)] * 3,
        out_shape=[jax.ShapeDtypeStruct((n, nsel), F32)] * 3,
        scratch_shapes=[pltpu.VMEM((nsub, PEER_NKEYS, tm), F32),
                        pltpu.VMEM((nsub * PEER_TOPK, tm), F32),
                        pltpu.VMEM((nsub * PEER_TOPK, tm), jnp.int32),
                        pltpu.VMEM((STAIR_PAD, tm), F32),
                        pltpu.VMEM((STAIR_PAD, tm), jnp.int32),
                        pltpu.VMEM((PEER_TOPK, tm), F32),
                        pltpu.VMEM((PEER_TOPK, tm), jnp.int32),
                        pltpu.VMEM((3, nsel, tm), F32)],
        compiler_params=_params(("parallel",)),
        name="peer_route",
    )(h16, wq16, keys16)


def _gelu(x):
    return 0.5 * x * (1.0 + lax.erf(x * (2.0 ** -0.5)))


def _experts_kernel(h_ref, h16_ref, i1_ref, i2_ref, gate_ref, u_ref, v_ref, lng_ref, lnb_ref,
                    o_ref, o16_ref, c_ref, acc_ref, *, alpha, stride):
    j = pl.program_id(1)
    tm = h_ref.shape[0]
    te = u_ref.shape[0]

    @pl.when(j == 0)
    def _():
        acc_ref[...] = jnp.zeros_like(acc_ref)
        iota = lax.broadcasted_iota(jnp.int32, (PEER_NKEYS, PEER_NKEYS), 0).astype(F32)

        def token(t, _):
            i1 = i1_ref[pl.ds(t, 1), :]
            i2 = i2_ref[pl.ds(t, 1), :]
            gt = gate_ref[pl.ds(t, 1), :]
            pt = jnp.where(iota == i1, 1.0, 0.0).astype(BF16)
            rt = jnp.where(iota == i2, gt, 0.0).astype(BF16)
            c_ref[pl.ds(t, PEER_NKEYS, stride=stride), :] = _dot_nt(pt, rt)
            return 0

        lax.fori_loop(0, tm, token, 0)

    a = _gelu(_dot_nt(h16_ref[...], u_ref[...]))
    groups = te // PEER_NKEYS
    ws = []
    for r in range(groups):
        start = pl.multiple_of((j * groups + r) * stride, SUBLANES)
        ws.append((a[:, r * PEER_NKEYS:(r + 1) * PEER_NKEYS]
                   * c_ref[pl.ds(start, tm), :]).astype(BF16))
    acc_ref[...] += _dot(jnp.concatenate(ws, axis=1), v_ref[...])

    @pl.when(j == pl.num_programs(1) - 1)
    def _():
        y = _layer_norm(alpha * h_ref[...] + acc_ref[...], lng_ref[...], lnb_ref[...])
        o_ref[...] = y
        o16_ref[...] = y.astype(BF16)


def _experts(h, h16, i1, i2, gate, u16, v16, lng, lnb, alpha, tm, te):
    n, d = h.shape
    ne = u16.shape[0]
    stride = tm + SUBLANES
    row = lambda i, j: (i, 0)
    const = lambda i, j: (0, 0)
    blk = lambda i, j: (j, 0)
    nsel = i1.shape[1]
    return pl.pallas_call(
        functools.partial(_experts_kernel, alpha=alpha, stride=stride),
        grid=(n // tm, ne // te),
        in_specs=[pl.BlockSpec((tm, d), row), pl.BlockSpec((tm, d), row),
                  pl.BlockSpec((tm, nsel), row), pl.BlockSpec((tm, nsel), row),
                  pl.BlockSpec((tm, nsel), row),
                  pl.BlockSpec((te, d), blk), pl.BlockSpec((te, d), blk),
                  pl.BlockSpec(lng.shape, const), pl.BlockSpec(lnb.shape, const)],
        out_specs=[pl.BlockSpec((tm, d), row)] * 2,
        out_shape=[jax.ShapeDtypeStruct((n, d), F32), jax.ShapeDtypeStruct((n, d), BF16)],
        scratch_shapes=[pltpu.VMEM((PEER_NKEYS * stride, PEER_NKEYS), F32),
                        pltpu.VMEM((tm, d), F32)],
        compiler_params=_params(("parallel", "arbitrary")),
        name="peer_experts",
    )(h, h16, i1, i2, gate, u16, v16, lng, lnb)


def _rope_tables(pos):
    half = DF_DIM // 2
    inv = ROPE_THETA ** (-jnp.arange(half, dtype=F32) / half)
    ang = pos[:, None] * inv[None, :]
    cos, sin = jnp.cos(ang), jnp.sin(ang)
    reps = LANES // DF_DIM
    return (jnp.tile(jnp.concatenate([cos, cos], axis=1), (1, reps)),
            jnp.tile(jnp.concatenate([-sin, sin], axis=1), (1, reps)))


def _token_stage(x, x16, attn, lp, tm_proj, tm_route, tm_exp, cos, sin, alpha):
    proj = _inproj(x16, lp["w_in"], lp["b_gate"], cos, sin, tm_proj)
    oa, ob = attn(proj)
    ga, gb = proj[10], proj[11]
    h, h16 = _merge(x, oa, ob, ga, gb, lp["w_br_a"], lp["w_br_b"], lp["w_o"], lp["ln1_g"], lp["ln1_b"],
                    alpha, tm_proj)
    i1, i2, gate = _route(h16, lp["peer_wq"], lp["peer_keys"], tm_route)
    y, y16 = _experts(h, h16, i1, i2, gate, lp["peer_u"], lp["peer_v"], lp["ln2_g"], lp["ln2_b"],
                      alpha, tm_exp, 1024)
    return y, y16, proj


def kernel(x_prompt, x_sample, cache_sb_k, cache_sb_v, cache_df_k, cache_df_v, page_table, meta_tokens, w_in, b_gate, w_br_a, w_br_b, w_o, lam_q1, lam_k1, lam_q2, lam_k2, df_norm_g, ln1_g, ln1_b, peer_wq, peer_sub_keys, peer_u, peer_v, ln2_g, ln2_b):
    depth, d = w_in.shape[0], w_in.shape[1]
    batch, seq, _ = x_prompt.shape
    nb, n_new, _ = x_sample.shape
    alpha = (2 * depth) ** 0.25
    t_real = seq + N_META
    t_pad = -(-t_real // Q_BLOCK) * Q_BLOCK
    n_pool, page = cache_sb_k.shape[1], cache_sb_k.shape[2]
    width = SB_HEADS * SB_DIM
    past_len = page_table.shape[1] * page

    meta = jnp.broadcast_to(meta_tokens[None].astype(F32), (batch, N_META, d))
    xp = jnp.concatenate([meta, x_prompt, jnp.zeros((batch, t_pad - t_real, d), F32)], axis=1)
    xp = xp.reshape(batch * t_pad, d)
    xp16 = xp.astype(BF16)
    xs = x_sample.reshape(nb * n_new, d)
    xs16 = xs.astype(BF16)
    cos_p, sin_p = _rope_tables(jnp.arange(t_pad, dtype=F32))
    cos_s, sin_s = _rope_tables(past_len + jnp.arange(n_new, dtype=F32))
    cos_s, sin_s = jnp.tile(cos_s, (nb, 1)), jnp.tile(sin_s, (nb, 1))
    caches = [c.reshape(depth, n_pool, page, width) for c in (cache_sb_k, cache_sb_v, cache_df_k, cache_df_v)]

    rows_p, rows_s = [], []
    for l in range(depth):
        lp = dict(
            w_in=w_in[l].astype(BF16), b_gate=b_gate[l].reshape(1, -1),
            w_br_a=w_br_a[l].astype(BF16), w_br_b=w_br_b[l].astype(BF16), w_o=w_o[l].astype(BF16),
            ln1_g=ln1_g[l].reshape(1, d), ln1_b=ln1_b[l].reshape(1, d),
            peer_wq=peer_wq[l].astype(BF16),
            peer_keys=peer_sub_keys[l].reshape(2 * PEER_HEADS, PEER_NKEYS, PEER_HALF).astype(BF16),
            peer_u=peer_u[l].astype(BF16), peer_v=peer_v[l].astype(BF16),
            ln2_g=ln2_g[l].reshape(1, d), ln2_b=ln2_b[l].reshape(1, d))
        lam_init = 0.8 - 0.6 * math.exp(-0.3 * l)
        lam = (jnp.exp(jnp.sum(lam_q1[l].astype(F32) * lam_k1[l].astype(F32)))
               - jnp.exp(jnp.sum(lam_q2[l].astype(F32) * lam_k2[l].astype(F32))) + lam_init).reshape(1)
        g_norm = df_norm_g[l].astype(F32)
        out_scale = 1.0 - lam_init

        def attn_prompt(proj):
            r3 = lambda a: a.reshape(batch, t_pad, width)
            oa = _sb_prefill(r3(proj[0]), r3(proj[6]), r3(proj[7]))
            ob = _df_prefill(lam, r3(proj[3]), r3(proj[8]), r3(proj[9]), g_norm, out_scale)
            return oa.reshape(-1, width), ob.reshape(-1, width)

        def attn_sample(proj):
            r3 = lambda a: a.reshape(nb, n_new, width)
            oa, ob = _decode_attention(l, page_table, lam, r3(proj[0]), r3(proj[1]), r3(proj[2]),
                                       r3(proj[3]), r3(proj[4]), r3(proj[5]), g_norm,
                                       *caches, out_scale)
            return oa.reshape(-1, width), ob.reshape(-1, width)

        xp, xp16, proj_p = _token_stage(xp, xp16, attn_prompt, lp, t_pad // 4, 256, 256, cos_p, sin_p, alpha)
        xs, xs16, proj_s = _token_stage(xs, xs16, attn_sample, lp, nb * n_new, 256, 256, cos_s, sin_s, alpha)
        rows_p.append([proj_p[i].reshape(batch, t_pad, width)[:, :t_real] for i in (1, 2, 4, 5)])
        rows_s.append([proj_s[i].reshape(nb, n_new, width) for i in (1, 2, 4, 5)])

    y_prompt = xp.reshape(batch, t_pad, d)[:, N_META:t_real]
    y_sample = xs.reshape(nb, n_new, d)

    def stack(rows, i, shape):
        return jnp.stack([r[i] for r in rows]).reshape(shape)

    tp, ts = (depth, batch, t_real), (depth, nb, n_new)
    return (y_prompt, y_sample,
            stack(rows_p, 0, tp + (SB_HEADS, SB_DIM)), stack(rows_p, 1, tp + (SB_HEADS, SB_DIM)),
            stack(rows_p, 2, tp + (DF_HEADS, 2, DF_DIM)), stack(rows_p, 3, tp + (DF_HEADS, 2 * DF_DIM)),
            stack(rows_s, 0, ts + (SB_HEADS, SB_DIM)), stack(rows_s, 1, ts + (SB_HEADS, SB_DIM)),
            stack(rows_s, 2, ts + (DF_HEADS, 2, DF_DIM)), stack(rows_s, 3, ts + (DF_HEADS, 2 * DF_DIM)))
```

```python
import functools
import math

import jax
import jax.numpy as jnp
import numpy as np
from jax import lax
from jax.experimental import pallas as pl
from jax.experimental.pallas import tpu as pltpu

F32 = jnp.float32
BF16 = jnp.bfloat16

LANES = 128
SUBLANES = 8
MXU_DIM = 256
VMEM_LIMIT = 56 * 1024 * 1024

N_META = 16
Q_BLOCK = 128
SB_HEADS = 8
SB_DIM = 64
DF_HEADS = 4
DF_DIM = 64
ROPE_THETA = 10000.0
LN_EPS = 1e-5
HEAD_NORM_EPS = 1e-5
PEER_HEADS = 8
PEER_NKEYS = 128
PEER_TOPK = 16
PEER_HALF = 128
NEG = float(np.finfo(np.float32).min)

STAIR = tuple((r1, PEER_TOPK // (r1 + 1)) for r1 in range(PEER_TOPK))
STAIR_ROWS = sum(n for _, n in STAIR)
STAIR_PAD = -(-STAIR_ROWS // SUBLANES) * SUBLANES


def _dot(a, b):
    return jnp.dot(a, b, preferred_element_type=F32)


def _dot_nt(a, b):
    return lax.dot_general(a, b, (((1,), (1,)), ((), ())), preferred_element_type=F32)


def _layer_norm(x, g, b):
    mu = jnp.mean(x, axis=-1, keepdims=True)
    xc = x - mu
    var = jnp.mean(xc * xc, axis=-1, keepdims=True)
    return xc * lax.rsqrt(var + LN_EPS) * g + b


def _params(sem, vmem=VMEM_LIMIT):
    return pltpu.CompilerParams(dimension_semantics=sem, vmem_limit_bytes=vmem)


def _rope(p, cos, sin):
    lane = lax.broadcasted_iota(jnp.int32, (p.shape[0], LANES), 1)
    first_half = (lane % DF_DIM) < (DF_DIM // 2)
    outs = []
    for s in range(p.shape[1] // LANES):
        slab = p[:, s * LANES:(s + 1) * LANES]
        rot = jnp.where(first_half, pltpu.roll(slab, LANES - DF_DIM // 2, 1),
                        pltpu.roll(slab, DF_DIM // 2, 1))
        outs.append(slab * cos + rot * sin)
    return jnp.concatenate(outs, axis=1)


def _inproj_kernel(x_ref, w_ref, bg_ref, cos_ref, sin_ref,
                   qsb_ref, ksb_ref, vsb_ref, qdf_ref, kdf_ref, vdf_ref,
                   ksb16_ref, vsb16_ref, kdf16_ref, vdf16_ref, ga_ref, gb_ref):
    x = x_ref[...]
    cos = cos_ref[...]
    sin = sin_ref[...]
    w = 512

    def proj(c0, c1):
        return _dot(x, w_ref[:, c0:c1])

    qsb_ref[...] = (proj(0, w) * (SB_DIM ** -0.5)).astype(BF16)
    k = proj(w, 2 * w)
    ksb_ref[...] = k
    ksb16_ref[...] = k.astype(BF16)
    v = proj(2 * w, 3 * w)
    vsb_ref[...] = v
    vsb16_ref[...] = v.astype(BF16)
    qdf_ref[...] = (_rope(proj(3 * w, 4 * w), cos, sin) * (DF_DIM ** -0.5)).astype(BF16)
    k = _rope(proj(4 * w, 5 * w), cos, sin)
    kdf_ref[...] = k
    kdf16_ref[...] = k.astype(BF16)
    v = proj(5 * w, 6 * w)
    vdf_ref[...] = v
    vdf16_ref[...] = v.astype(BF16)
    d = ga_ref.shape[1]
    ga_ref[...] = jax.nn.sigmoid(proj(6 * w, 6 * w + d) + bg_ref[:, :d]).astype(BF16)
    gb_ref[...] = jax.nn.sigmoid(proj(6 * w + d, 6 * w + 2 * d) + bg_ref[:, d:]).astype(BF16)


def _inproj(x16, w16, bg, cos, sin, tm):
    n, d = x16.shape
    nblk = cos.shape[0] // tm
    row = lambda i: (i, 0)
    const = lambda i: (0, 0)
    tab = lambda i: (i % nblk, 0)
    wide = 512
    out_shape = ([jax.ShapeDtypeStruct((n, wide), BF16)] + [jax.ShapeDtypeStruct((n, wide), F32)] * 2
                 + [jax.ShapeDtypeStruct((n, wide), BF16)] + [jax.ShapeDtypeStruct((n, wide), F32)] * 2
                 + [jax.ShapeDtypeStruct((n, wide), BF16)] * 4 + [jax.ShapeDtypeStruct((n, d), BF16)] * 2)
    out_specs = [pl.BlockSpec((tm, wide), row)] * 10 + [pl.BlockSpec((tm, d), row)] * 2
    return pl.pallas_call(
        _inproj_kernel,
        grid=(n // tm,),
        in_specs=[pl.BlockSpec((tm, d), row),
                  pl.BlockSpec(w16.shape, const),
                  pl.BlockSpec(bg.shape, const),
                  pl.BlockSpec((tm, LANES), tab),
                  pl.BlockSpec((tm, LANES), tab)],
        out_specs=out_specs,
        out_shape=out_shape,
        compiler_params=_params(("parallel",)),
        name="inproj",
    )(x16, w16, bg, cos, sin)


def _cumsum_mats(n):
    r = lax.broadcasted_iota(jnp.int32, (n, n), 0)
    c = lax.broadcasted_iota(jnp.int32, (n, n), 1)
    l = (r > c).astype(BF16)
    return jnp.concatenate([l, l], axis=0)


def _sb_scores(z, mask):
    e = jnp.exp(-jnp.abs(z))
    r = 1.0 / (1.0 + e)
    lk = jnp.log(r) - jnp.maximum(z, 0.0)
    sig = jnp.where(z >= 0.0, r, e * r)
    if mask is not None:
        lk = jnp.where(mask, lk, 0.0)
        sig = jnp.where(mask, sig, 0.0)
    return lk, sig


def _sums_to_right(lk, l2):
    hi = lk.astype(BF16)
    lo = (lk - hi.astype(F32)).astype(BF16)
    return _dot(jnp.concatenate([hi, lo], axis=1), l2)


def _sb_weights(z, mask, l2, csum):
    lk, sig = _sb_scores(z, mask)
    cum = _sums_to_right(lk, l2)
    a = sig * jnp.exp(csum + cum)
    return a, csum + cum[:, :1] + lk[:, :1]


def _masked_heads(q, n_heads, dim):
    lane = lax.broadcasted_iota(jnp.int32, (q.shape[0], LANES), 1)
    per = LANES // dim
    out = []
    for h in range(n_heads):
        slab = q[:, (h // per) * LANES:(h // per + 1) * LANES]
        out.append(jnp.where(lane // dim == h % per, slab, 0.0).astype(BF16))
    return out


def _top_group(qi, tq, tk, t):
    nominal = (qi * tq // tk) * tk
    start = pl.multiple_of(jnp.minimum(nominal, t - tk), tq)
    key = start + lax.broadcasted_iota(jnp.int32, (tq, tk), 1)
    pos = qi * tq + lax.broadcasted_iota(jnp.int32, (tq, tk), 0)
    in_group = key >= nominal
    return start, jnp.logical_and(in_group, key < pos), jnp.logical_and(in_group, key <= pos)


def _sb_prefill_kernel(q_ref, k_ref, v_ref, o_ref, *, tq, tk):
    qi = pl.program_id(1)
    t, width = k_ref.shape[1], q_ref.shape[2]
    nh = width // SB_DIM
    pairs = nh // 2
    qms = _masked_heads(q_ref[0].astype(F32), nh, SB_DIM)
    q2 = [jnp.concatenate(qms[2 * p:2 * p + 2], axis=0) for p in range(pairs)]
    l2 = _cumsum_mats(tk)

    def group(start, mask, csums, accs):
        zs = []
        for p in range(pairs):
            z2 = _dot_nt(q2[p], k_ref[0, pl.ds(start, tk), p * LANES:(p + 1) * LANES])
            zs += [z2[:tq], z2[tq:]]
        scores = [_sb_scores(z, mask) for z in zs]
        cums = _sums_to_right(jnp.concatenate([lk for lk, _ in scores], axis=0), l2)
        new_c, a16 = [], []
        for h, (lk, sig) in enumerate(scores):
            cum = cums[h * tq:(h + 1) * tq]
            a16.append((sig * jnp.exp(csums[h] + cum)).astype(BF16))
            new_c.append(csums[h] + cum[:, :1] + lk[:, :1])
        new_a = []
        for p in range(pairs):
            o2 = _dot(jnp.concatenate(a16[2 * p:2 * p + 2], axis=0),
                      v_ref[0, pl.ds(start, tk), p * LANES:(p + 1) * LANES])
            new_a += [accs[2 * p] + o2[:tq], accs[2 * p + 1] + o2[tq:]]
        return tuple(new_c), tuple(new_a)

    zero_c = tuple(jnp.zeros((tq, 1), F32) for _ in range(nh))
    zero_a = tuple(jnp.zeros((tq, LANES), F32) for _ in range(nh))
    start, strict, _ = _top_group(qi, tq, tk, t)
    carry = group(start, strict, zero_c, zero_a)
    below = qi * tq // tk

    def body(j, carry):
        return group(pl.multiple_of((below - 1 - j) * tk, tk), None, *carry)

    _, accs = lax.fori_loop(0, below, body, carry)
    lane = lax.broadcasted_iota(jnp.int32, (tq, LANES), 1)
    o_ref[0] = jnp.concatenate(
        [jnp.where(lane < SB_DIM, accs[2 * p], accs[2 * p + 1]) for p in range(pairs)],
        axis=1).astype(o_ref.dtype)


def _sb_prefill(q16, k16, v16, tq=Q_BLOCK, tk=2 * Q_BLOCK):
    b, t, w = q16.shape
    blk_q = pl.BlockSpec((1, tq, w), lambda bi, qi: (bi, qi, 0))
    blk_kv = pl.BlockSpec((1, t, w), lambda bi, qi: (bi, 0, 0))
    return pl.pallas_call(
        functools.partial(_sb_prefill_kernel, tq=tq, tk=tk),
        grid=(b, t // tq),
        in_specs=[blk_q, blk_kv, blk_kv],
        out_specs=blk_q,
        out_shape=jax.ShapeDtypeStruct((b, t, w), BF16),
        compiler_params=_params(("parallel", "arbitrary")),
        name="sb_prefill",
    )(q16, k16, v16)


def _head_norm(o, g, out_scale):
    ms = jnp.mean(o * o, axis=-1, keepdims=True)
    return o * lax.rsqrt(ms + HEAD_NORM_EPS) * g * out_scale


def _df_prefill_kernel(lam_ref, q_ref, k_ref, v_ref, g_ref, o_ref, *, tq, tk, out_scale):
    qi = pl.program_id(1)
    t, width = k_ref.shape[1], q_ref.shape[2]
    nh = width // LANES
    qms = _masked_heads(q_ref[0].astype(F32), 2 * nh, DF_DIM)
    q2 = [jnp.concatenate(qms[2 * h:2 * h + 2], axis=0) for h in range(nh)]

    def scores(start, mask):
        out = []
        for h in range(nh):
            s = _dot_nt(q2[h], k_ref[0, pl.ds(start, tk), h * LANES:(h + 1) * LANES])
            if mask is not None:
                s = jnp.concatenate([jnp.where(mask, s[:tq], NEG), jnp.where(mask, s[tq:], NEG)], axis=0)
            out.append(s)
        return out

    def values(start, ps):
        return [_dot(ps[h].astype(BF16), v_ref[0, pl.ds(start, tk), h * LANES:(h + 1) * LANES])
                for h in range(nh)]

    start, _, inclusive = _top_group(qi, tq, tk, t)
    ss = scores(start, inclusive)
    ms = [jnp.max(s, axis=1, keepdims=True) for s in ss]
    ps = [jnp.exp(s - m) for s, m in zip(ss, ms)]
    ls = [jnp.sum(p, axis=1, keepdims=True) for p in ps]
    accs = values(start, ps)

    def body(j, carry):
        ms, ls, accs = carry
        start = pl.multiple_of(j * tk, tk)
        ss = scores(start, None)
        nm = [jnp.maximum(m, jnp.max(s, axis=1, keepdims=True)) for s, m in zip(ss, ms)]
        alphas = [jnp.exp(m - n) for m, n in zip(ms, nm)]
        ps = [jnp.exp(s - n) for s, n in zip(ss, nm)]
        nl = [a * l + jnp.sum(p, axis=1, keepdims=True) for a, l, p in zip(alphas, ls, ps)]
        na = [a * acc + o for a, acc, o in zip(alphas, accs, values(start, ps))]
        return tuple(nm), tuple(nl), tuple(na)

    _, ls, accs = lax.fori_loop(0, qi * tq // tk, body, (tuple(ms), tuple(ls), tuple(accs)))
    outs = []
    for h in range(nh):
        o = accs[h] / ls[h]
        outs.append(_head_norm(o[:tq] - lam_ref[0] * o[tq:], g_ref[h:h + 1, :], out_scale))
    o_ref[0] = jnp.concatenate(outs, axis=1).astype(o_ref.dtype)


def _df_prefill(lam, q16, k16, v16, g, out_scale, tq=Q_BLOCK, tk=2 * Q_BLOCK):
    b, t, w = q16.shape
    blk_q = pl.BlockSpec((1, tq, w), lambda bi, qi: (bi, qi, 0))
    blk_kv = pl.BlockSpec((1, t, w), lambda bi, qi: (bi, 0, 0))
    return pl.pallas_call(
        functools.partial(_df_prefill_kernel, tq=tq, tk=tk, out_scale=out_scale),
        grid=(b, t // tq),
        in_specs=[pl.BlockSpec(memory_space=pltpu.SMEM), blk_q, blk_kv, blk_kv,
                  pl.BlockSpec(g.shape, lambda bi, qi: (0, 0))],
        out_specs=blk_q,
        out_shape=jax.ShapeDtypeStruct((b, t, w), BF16),
        compiler_params=_params(("parallel", "arbitrary")),
        name="df_prefill",
    )(lam, q16, k16, v16, g)


def _decode_kernel(pt_ref, lam_ref, qsb_ref, ksbn_ref, vsbn_ref, qdf_ref, kdfn_ref, vdfn_ref, g_ref,
                   *rest, pg, n_new, out_scale):
    del pt_ref
    sbk = rest[0:pg]
    sbv = rest[pg:2 * pg]
    dfk = rest[2 * pg:3 * pg]
    dfv = rest[3 * pg:4 * pg]
    oa_ref, ob_ref = rest[4 * pg:4 * pg + 2]
    qxa_ref, qxb_ref, csum_ref, acca_ref, m_ref, l_ref, accb_ref = rest[4 * pg + 2:]
    c = pl.program_id(1)
    nrow, width = qxa_ref.shape
    page = sbk[0].shape[1]
    l2 = _cumsum_mats(page)

    @pl.when(c == 0)
    def _():
        lane = lax.broadcasted_iota(jnp.int32, (n_new, width), 1)
        qa = qsb_ref[...].astype(F32)
        qb = qdf_ref[...].astype(F32)
        for h in range(nrow // n_new):
            sel = (lane // (width * n_new // nrow)) == h
            qxa_ref[h * n_new:(h + 1) * n_new, :] = jnp.where(sel, qa, 0.0)
            qxb_ref[h * n_new:(h + 1) * n_new, :] = jnp.where(sel, qb, 0.0)
        pad = jnp.zeros((page - n_new, width), F32)
        tok = lax.broadcasted_iota(jnp.int32, (nrow, page), 0) % n_new
        key = lax.broadcasted_iota(jnp.int32, (nrow, page), 1)
        kb = jnp.concatenate([ksbn_ref[...], pad], axis=0).astype(BF16)
        vb = jnp.concatenate([vsbn_ref[...], pad], axis=0).astype(BF16)
        a, csum = _sb_weights(_dot_nt(qxa_ref[...].astype(BF16), kb), key < tok, l2,
                              jnp.zeros((nrow, 1), F32))
        csum_ref[...] = csum
        acca_ref[...] = _dot(a.astype(BF16), vb)
        kb = jnp.concatenate([kdfn_ref[...], pad], axis=0).astype(BF16)
        vb = jnp.concatenate([vdfn_ref[...], pad], axis=0).astype(BF16)
        s = jnp.where(key <= tok, _dot_nt(qxb_ref[...].astype(BF16), kb), NEG)
        m = jnp.max(s, axis=1, keepdims=True)
        p = jnp.exp(s - m)
        m_ref[...] = m
        l_ref[...] = jnp.sum(p, axis=1, keepdims=True)
        accb_ref[...] = _dot(p.astype(BF16), vb)

    def slots(refs):
        return jnp.concatenate([r[...].astype(BF16) for r in refs], axis=1)

    z = _dot(qxa_ref[...].astype(BF16), slots(sbk))
    lk, sig = _sb_scores(z, None)
    cols = [slice(g * page, (g + 1) * page) for g in range(pg)]
    cums = _sums_to_right(jnp.concatenate([lk[:, cs] for cs in cols], axis=0), l2)
    csum = csum_ref[...]
    a16 = []
    for g, cs in enumerate(cols):
        cum = cums[g * nrow:(g + 1) * nrow]
        a16.append((sig[:, cs] * jnp.exp(csum + cum)).astype(BF16))
        csum = csum + cum[:, :1] + lk[:, cs][:, :1]
    csum_ref[...] = csum
    acca = acca_ref[...] + _dot_nt(jnp.concatenate(a16, axis=1), slots(sbv))
    acca_ref[...] = acca

    s = _dot(qxb_ref[...].astype(BF16), slots(dfk))
    m_old = m_ref[...]
    m = jnp.maximum(m_old, jnp.max(s, axis=1, keepdims=True))
    alpha = jnp.exp(m_old - m)
    p = jnp.exp(s - m)
    l = alpha * l_ref[...] + jnp.sum(p, axis=1, keepdims=True)
    vb = jnp.concatenate(
        [jnp.concatenate([r[pl.ds(h, page, stride=DF_HEADS), :] for h in range(DF_HEADS)],
                         axis=1).astype(BF16) for r in dfv],
        axis=0)
    accb = alpha * accb_ref[...] + _dot(p.astype(BF16), vb)
    m_ref[...] = m
    l_ref[...] = l
    accb_ref[...] = accb

    @pl.when(c == pl.num_programs(1) - 1)
    def _():
        lane = lax.broadcasted_iota(jnp.int32, (n_new, width), 1)
        out = jnp.zeros((n_new, width), F32)
        for h in range(SB_HEADS):
            out = out + jnp.where(lane // SB_DIM == h, acca[h * n_new:(h + 1) * n_new, :], 0.0)
        oa_ref[...] = out.astype(oa_ref.dtype)
        o = accb / l
        for h in range(DF_HEADS):
            lanes = slice(h * LANES, (h + 1) * LANES)
            o0 = o[(2 * h) * n_new:(2 * h + 1) * n_new, lanes]
            o1 = o[(2 * h + 1) * n_new:(2 * h + 2) * n_new, lanes]
            oh = _head_norm(o0 - lam_ref[0] * o1, g_ref[h:h + 1, :], out_scale)
            ob_ref[:, lanes] = oh.astype(ob_ref.dtype)


def _decode_attention(layer, page_table, lam, qsb, ksbn, vsbn, qdf, kdfn, vdfn, g,
                      c_sbk, c_sbv, c_dfk, c_dfv, out_scale, pg=8):
    nb, n_new, width = qsb.shape
    n_pages = page_table.shape[1]
    page = c_sbk.shape[3]
    new_spec = pl.BlockSpec((None, n_new, width), lambda b, c, pt: (b, 0, 0))

    def page_specs(block):
        def spec(g_):
            def imap(b, c, pt):
                return (layer, pt[b, n_pages - 1 - (c * pg + g_)], 0, 0)
            return pl.BlockSpec((None, None) + block, imap)
        return [spec(g_) for g_ in range(pg)]

    t_specs = page_specs((width, page))
    v_specs = page_specs((page * DF_HEADS, LANES))
    nrow = SB_HEADS * n_new
    grid_spec = pltpu.PrefetchScalarGridSpec(
        num_scalar_prefetch=1,
        grid=(nb, n_pages // pg),
        in_specs=[pl.BlockSpec(memory_space=pltpu.SMEM)] + [new_spec] * 6
                 + [pl.BlockSpec(g.shape, lambda b, c, pt: (0, 0))] + t_specs * 3 + v_specs,
        out_specs=[new_spec, new_spec],
        scratch_shapes=[pltpu.VMEM((nrow, width), F32), pltpu.VMEM((nrow, width), F32),
                        pltpu.VMEM((nrow, 1), F32), pltpu.VMEM((nrow, width), F32),
                        pltpu.VMEM((nrow, 1), F32), pltpu.VMEM((nrow, 1), F32),
                        pltpu.VMEM((nrow, width), F32)],
    )
    return pl.pallas_call(
        functools.partial(_decode_kernel, pg=pg, n_new=n_new, out_scale=out_scale),
        grid_spec=grid_spec,
        out_shape=[jax.ShapeDtypeStruct((nb, n_new, width), BF16)] * 2,
        compiler_params=_params(("parallel", "arbitrary")),
        name="decode_attention",
    )(page_table, lam, qsb, ksbn, vsbn, qdf, kdfn, vdfn, g,
      *([c_sbk] * pg), *([c_sbv] * pg), *([c_dfk] * pg), *([c_dfv] * pg))


def _merge_kernel(x_ref, oa_ref, ob_ref, ga_ref, gb_ref, wa_ref, wb_ref, wo_ref, g_ref, b_ref,
                  h_ref, h16_ref, *, alpha):
    merged = (ga_ref[...].astype(F32) * _dot(oa_ref[...], wa_ref[...])
              + gb_ref[...].astype(F32) * _dot(ob_ref[...], wb_ref[...]))
    y = alpha * x_ref[...] + _dot(merged.astype(BF16), wo_ref[...])
    h = _layer_norm(y, g_ref[...], b_ref[...])
    h_ref[...] = h
    h16_ref[...] = h.astype(BF16)


def _merge(x, oa, ob, ga, gb, wa, wb, wo, g, b, alpha, tm):
    n, d = x.shape
    row = lambda i: (i, 0)
    const = lambda i: (0, 0)
    return pl.pallas_call(
        functools.partial(_merge_kernel, alpha=alpha),
        grid=(n // tm,),
        in_specs=[pl.BlockSpec((tm, d), row), pl.BlockSpec((tm, oa.shape[1]), row),
                  pl.BlockSpec((tm, ob.shape[1]), row), pl.BlockSpec((tm, d), row),
                  pl.BlockSpec((tm, d), row), pl.BlockSpec(wa.shape, const),
                  pl.BlockSpec(wb.shape, const), pl.BlockSpec(wo.shape, const),
                  pl.BlockSpec(g.shape, const), pl.BlockSpec(b.shape, const)],
        out_specs=[pl.BlockSpec((tm, d), row)] * 2,
        out_shape=[jax.ShapeDtypeStruct((n, d), F32), jax.ShapeDtypeStruct((n, d), BF16)],
        compiler_params=_params(("parallel",)),
        name="merge_ln",
    )(x, oa, ob, ga, gb, wa, wb, wo, g, b)


def _extract_topk(s, k, val_ref, idx_ref, row0):
    n = s.shape[0]
    iota = lax.broadcasted_iota(jnp.int32, s.shape, 0)
    for r in range(k):
        m = jnp.max(s, axis=0, keepdims=True)
        idx = jnp.min(jnp.where(s == m, iota, n), axis=0, keepdims=True)
        val_ref[pl.ds(row0 + r, 1), :] = m
        idx_ref[pl.ds(row0 + r, 1), :] = idx
        s = jnp.where(iota == idx, -jnp.inf, s)


def _route_kernel(h_ref, wq_ref, keys_ref, i1_ref, i2_ref, g_ref,
                  s_ref, v_ref, i_ref, cand_ref, ci_ref, best_ref, pos_ref, sel_ref):
    tm = h_ref.shape[0]
    nsub = 2 * PEER_HEADS
    q = _dot(h_ref[...], wq_ref[...]).astype(BF16)
    for hc in range(nsub):
        s_ref[hc] = _dot_nt(keys_ref[hc], q[:, hc * PEER_HALF:(hc + 1) * PEER_HALF])

    def level1(hc, _):
        _extract_topk(s_ref[hc], PEER_TOPK, v_ref, i_ref, pl.multiple_of(hc * PEER_TOPK, PEER_TOPK))
        return 0

    lax.fori_loop(0, nsub, level1, 0)

    cand_ref[STAIR_ROWS:, :] = jnp.full((STAIR_PAD - STAIR_ROWS, tm), -jnp.inf, F32)
    ci_ref[STAIR_ROWS:, :] = jnp.zeros((STAIR_PAD - STAIR_ROWS, tm), jnp.int32)

    def level2(h, _):
        base1 = pl.multiple_of(2 * h * PEER_TOPK, PEER_TOPK)
        base2 = pl.multiple_of((2 * h + 1) * PEER_TOPK, PEER_TOPK)
        off = 0
        for r1, n2 in STAIR:
            cand_ref[off:off + n2, :] = v_ref[pl.ds(base1 + r1, 1), :] + v_ref[pl.ds(base2, n2), :]
            ci_ref[off:off + n2, :] = (i_ref[pl.ds(base1 + r1, 1), :] * PEER_NKEYS
                                       + i_ref[pl.ds(base2, n2), :])
            off += n2
        _extract_topk(cand_ref[...], PEER_TOPK, best_ref, pos_ref, 0)
        best = best_ref[...]
        e = jnp.exp(best - best[:1, :])
        gate = e / jnp.sum(e, axis=0, keepdims=True)
        ci = ci_ref[...]
        iota = lax.broadcasted_iota(jnp.int32, ci.shape, 0)
        out0 = pl.multiple_of(h * PEER_TOPK, PEER_TOPK)
        for r in range(PEER_TOPK):
            eidx = jnp.max(jnp.where(iota == pos_ref[r:r + 1, :], ci, -1), axis=0, keepdims=True)
            sel_ref[0, pl.ds(out0 + r, 1), :] = (eidx >> 7).astype(F32)
            sel_ref[1, pl.ds(out0 + r, 1), :] = (eidx & (PEER_NKEYS - 1)).astype(F32)
        sel_ref[2, pl.ds(out0, PEER_TOPK), :] = gate
        return 0

    lax.fori_loop(0, PEER_HEADS, level2, 0)
    for c0 in range(0, tm, LANES):
        i1_ref[c0:c0 + LANES, :] = sel_ref[0, :, c0:c0 + LANES].T
        i2_ref[c0:c0 + LANES, :] = sel_ref[1, :, c0:c0 + LANES].T
        g_ref[c0:c0 + LANES, :] = sel_ref[2, :, c0:c0 + LANES].T


def _route(h16, wq16, keys16, tm):
    n, d = h16.shape
    nsub = 2 * PEER_HEADS
    nsel = PEER_HEADS * PEER_TOPK
    row = lambda i: (i, 0)
    return pl.pallas_call(
        _route_kernel,
        grid=(n // tm,),
        in_specs=[pl.BlockSpec((tm, d), row), pl.BlockSpec(wq16.shape, lambda i: (0, 0)),
                  pl.BlockSpec(keys16.shape, lambda i: (0, 0, 0))],
        out_specs=[pl.BlockSpec((tm, nsel), row)] * 3,
        out_shape=[jax.ShapeDtypeStruct((n, nsel), F32)] * 3,
        scratch_shapes=[pltpu.VMEM((nsub, PEER_NKEYS, tm), F32),
                        pltpu.VMEM((nsub * PEER_TOPK, tm), F32),
                        pltpu.VMEM((nsub * PEER_TOPK, tm), jnp.int32),
                        pltpu.VMEM((STAIR_PAD, tm), F32),
                        pltpu.VMEM((STAIR_PAD, tm), jnp.int32),
                        pltpu.VMEM((PEER_TOPK, tm), F32),
                        pltpu.VMEM((PEER_TOPK, tm), jnp.int32),
                        pltpu.VMEM((3, nsel, tm), F32)],
        compiler_params=_params(("parallel",)),
        name="peer_route",
    )(h16, wq16, keys16)


TOKENS_PER_STEP = 16


def _gelu(x):
    return 0.5 * x * (1.0 + lax.erf(x * (2.0 ** -0.5)))


def _experts_kernel(h_ref, h16_ref, i1_ref, i2_ref, gate_ref, u_ref, v_ref, lng_ref, lnb_ref,
                    o_ref, o16_ref, c_ref, acc_ref, *, alpha, stride):
    j = pl.program_id(1)
    tm = h_ref.shape[0]
    te = u_ref.shape[0]

    @pl.when(j == 0)
    def _():
        acc_ref[...] = jnp.zeros_like(acc_ref)
        iota = lax.broadcasted_iota(jnp.int32, (PEER_NKEYS, PEER_NKEYS), 0).astype(F32)

        def tokens(i, _):
            t0 = i * TOKENS_PER_STEP
            pts, rts = [], []
            for u in range(TOKENS_PER_STEP):
                i1 = i1_ref[pl.ds(t0 + u, 1), :]
                i2 = i2_ref[pl.ds(t0 + u, 1), :]
                gt = gate_ref[pl.ds(t0 + u, 1), :]
                pts.append(jnp.where(iota == i1, 1.0, 0.0).astype(BF16))
                rts.append(jnp.where(iota == i2, gt, 0.0).astype(BF16))
            tiles = [_dot_nt(pt, rt) for pt, rt in zip(pts, rts)]
            for u, tile in enumerate(tiles):
                c_ref[pl.ds(t0 + u, PEER_NKEYS, stride=stride), :] = tile
            return 0

        lax.fori_loop(0, tm // TOKENS_PER_STEP, tokens, 0)

    x = h16_ref[...]
    per = MXU_DIM // PEER_NKEYS
    y = None
    for r in range(te // MXU_DIM):
        rows = slice(r * MXU_DIM, (r + 1) * MXU_DIM)
        a = _gelu(_dot_nt(x, u_ref[rows, :]))
        first = (j * (te // MXU_DIM) + r) * per
        c = jnp.concatenate(
            [c_ref[pl.ds(pl.multiple_of((first + s) * stride, SUBLANES), tm), :] for s in range(per)],
            axis=1)
        part = _dot((a * c).astype(BF16), v_ref[rows, :])
        y = part if y is None else y + part
    acc_ref[...] += y

    @pl.when(j == pl.num_programs(1) - 1)
    def _():
        out = _layer_norm(alpha * h_ref[...] + acc_ref[...], lng_ref[...], lnb_ref[...])
        o_ref[...] = out
        o16_ref[...] = out.astype(BF16)


def _experts(h, h16, i1, i2, gate, u16, v16, lng, lnb, alpha, tm, te):
    n, d = h.shape
    ne = u16.shape[0]
    stride = tm + SUBLANES
    row = lambda i, j: (i, 0)
    const = lambda i, j: (0, 0)
    blk = lambda i, j: (j, 0)
    nsel = i1.shape[1]
    return pl.pallas_call(
        functools.partial(_experts_kernel, alpha=alpha, stride=stride),
        grid=(n // tm, ne // te),
        in_specs=[pl.BlockSpec((tm, d), row), pl.BlockSpec((tm, d), row),
                  pl.BlockSpec((tm, nsel), row), pl.BlockSpec((tm, nsel), row),
                  pl.BlockSpec((tm, nsel), row),
                  pl.BlockSpec((te, d), blk), pl.BlockSpec((te, d), blk),
                  pl.BlockSpec(lng.shape, const), pl.BlockSpec(lnb.shape, const)],
        out_specs=[pl.BlockSpec((tm, d), row)] * 2,
        out_shape=[jax.ShapeDtypeStruct((n, d), F32), jax.ShapeDtypeStruct((n, d), BF16)],
        scratch_shapes=[pltpu.VMEM((PEER_NKEYS * stride, PEER_NKEYS), F32),
                        pltpu.VMEM((tm, d), F32)],
        compiler_params=_params(("parallel", "arbitrary")),
        name="peer_experts",
    )(h, h16, i1, i2, gate, u16, v16, lng, lnb)


def _rope_tables(pos):
    half = DF_DIM // 2
    inv = ROPE_THETA ** (-jnp.arange(half, dtype=F32) / half)
    ang = pos[:, None] * inv[None, :]
    cos, sin = jnp.cos(ang), jnp.sin(ang)
    reps = LANES // DF_DIM
    return (jnp.tile(jnp.concatenate([cos, cos], axis=1), (1, reps)),
            jnp.tile(jnp.concatenate([-sin, sin], axis=1), (1, reps)))


def _token_stage(x, x16, attn, lp, tm_proj, tm_route, tm_exp, cos, sin, alpha):
    proj = _inproj(x16, lp["w_in"], lp["b_gate"], cos, sin, tm_proj)
    oa, ob = attn(proj)
    ga, gb = proj[10], proj[11]
    h, h16 = _merge(x, oa, ob, ga, gb, lp["w_br_a"], lp["w_br_b"], lp["w_o"], lp["ln1_g"], lp["ln1_b"],
                    alpha, tm_proj)
    i1, i2, gate = _route(h16, lp["peer_wq"], lp["peer_keys"], tm_route)
    y, y16 = _experts(h, h16, i1, i2, gate, lp["peer_u"], lp["peer_v"], lp["ln2_g"], lp["ln2_b"],
                      alpha, tm_exp, 2048)
    return y, y16, proj


def kernel(x_prompt, x_sample, cache_sb_k, cache_sb_v, cache_df_k, cache_df_v, page_table, meta_tokens, w_in, b_gate, w_br_a, w_br_b, w_o, lam_q1, lam_k1, lam_q2, lam_k2, df_norm_g, ln1_g, ln1_b, peer_wq, peer_sub_keys, peer_u, peer_v, ln2_g, ln2_b):
    depth, d = w_in.shape[0], w_in.shape[1]
    batch, seq, _ = x_prompt.shape
    nb, n_new, _ = x_sample.shape
    alpha = (2 * depth) ** 0.25
    t_real = seq + N_META
    t_pad = -(-t_real // Q_BLOCK) * Q_BLOCK
    n_pool, page = cache_sb_k.shape[1], cache_sb_k.shape[2]
    width = SB_HEADS * SB_DIM
    past_len = page_table.shape[1] * page

    meta = jnp.broadcast_to(meta_tokens[None].astype(F32), (batch, N_META, d))
    xp = jnp.concatenate([meta, x_prompt, jnp.zeros((batch, t_pad - t_real, d), F32)], axis=1)
    xp = xp.reshape(batch * t_pad, d)
    xp16 = xp.astype(BF16)
    xs = x_sample.reshape(nb * n_new, d)
    xs16 = xs.astype(BF16)
    cos_p, sin_p = _rope_tables(jnp.arange(t_pad, dtype=F32))
    cos_s, sin_s = _rope_tables(past_len + jnp.arange(n_new, dtype=F32))
    cos_s, sin_s = jnp.tile(cos_s, (nb, 1)), jnp.tile(sin_s, (nb, 1))
    caches = [jnp.moveaxis(c, 2, -1).reshape(depth, n_pool, width, page)
              for c in (cache_sb_k, cache_sb_v, cache_df_k)]
    caches.append(cache_df_v.reshape(depth, n_pool, page * DF_HEADS, 2 * DF_DIM))

    rows_p, rows_s = [], []
    for l in range(depth):
        lp = dict(
            w_in=w_in[l].astype(BF16), b_gate=b_gate[l].reshape(1, -1),
            w_br_a=w_br_a[l].astype(BF16), w_br_b=w_br_b[l].astype(BF16), w_o=w_o[l].astype(BF16),
            ln1_g=ln1_g[l].reshape(1, d), ln1_b=ln1_b[l].reshape(1, d),
            peer_wq=peer_wq[l].astype(BF16),
            peer_keys=peer_sub_keys[l].reshape(2 * PEER_HEADS, PEER_NKEYS, PEER_HALF).astype(BF16),
            peer_u=peer_u[l].astype(BF16), peer_v=peer_v[l].astype(BF16),
            ln2_g=ln2_g[l].reshape(1, d), ln2_b=ln2_b[l].reshape(1, d))
        lam_init = 0.8 - 0.6 * math.exp(-0.3 * l)
        lam = (jnp.exp(jnp.sum(lam_q1[l].astype(F32) * lam_k1[l].astype(F32)))
               - jnp.exp(jnp.sum(lam_q2[l].astype(F32) * lam_k2[l].astype(F32))) + lam_init).reshape(1)
        g_norm = df_norm_g[l].astype(F32)
        out_scale = 1.0 - lam_init

        def attn_prompt(proj):
            r3 = lambda a: a.reshape(batch, t_pad, width)
            oa = _sb_prefill(r3(proj[0]), r3(proj[6]), r3(proj[7]))
            ob = _df_prefill(lam, r3(proj[3]), r3(proj[8]), r3(proj[9]), g_norm, out_scale)
            return oa.reshape(-1, width), ob.reshape(-1, width)

        def attn_sample(proj):
            r3 = lambda a: a.reshape(nb, n_new, width)
            oa, ob = _decode_attention(l, page_table, lam, r3(proj[0]), r3(proj[1]), r3(proj[2]),
                                       r3(proj[3]), r3(proj[4]), r3(proj[5]), g_norm,
                                       *caches, out_scale)
            return oa.reshape(-1, width), ob.reshape(-1, width)

        xp, xp16, proj_p = _token_stage(xp, xp16, attn_prompt, lp, t_pad // 4, 256, 256, cos_p, sin_p, alpha)
        xs, xs16, proj_s = _token_stage(xs, xs16, attn_sample, lp, nb * n_new, 256, 256, cos_s, sin_s, alpha)
        rows_p.append([proj_p[i].reshape(batch, t_pad, width)[:, :t_real] for i in (1, 2, 4, 5)])
        rows_s.append([proj_s[i].reshape(nb, n_new, width) for i in (1, 2, 4, 5)])

    y_prompt = xp.reshape(batch, t_pad, d)[:, N_META:t_real]
    y_sample = xs.reshape(nb, n_new, d)

    def stack(rows, i, shape):
        return jnp.stack([r[i] for r in rows]).reshape(shape)

    tp, ts = (depth, batch, t_real), (depth, nb, n_new)
    return (y_prompt, y_sample,
            stack(rows_p, 0, tp + (SB_HEADS, SB_DIM)), stack(rows_p, 1, tp + (SB_HEADS, SB_DIM)),
            stack(rows_p, 2, tp + (DF_HEADS, 2, DF_DIM)), stack(rows_p, 3, tp + (DF_HEADS, 2 * DF_DIM)),
            stack(rows_s, 0, ts + (SB_HEADS, SB_DIM)), stack(rows_s, 1, ts + (SB_HEADS, SB_DIM)),
            stack(rows_s, 2, ts + (DF_HEADS, 2, DF_DIM)), stack(rows_s, 3, ts + (DF_HEADS, 2 * DF_DIM)))
```

```python
import functools
import math

import jax
import jax.numpy as jnp
import numpy as np
from jax import lax
from jax.experimental import pallas as pl
from jax.experimental.pallas import tpu as pltpu

F32 = jnp.float32
BF16 = jnp.bfloat16

LANES = 128
SUBLANES = 8
MXU_DIM = 256
VMEM_LIMIT = 56 * 1024 * 1024

N_META = 16
Q_BLOCK = 128
SB_HEADS = 8
SB_DIM = 64
DF_HEADS = 4
DF_DIM = 64
ROPE_THETA = 10000.0
LN_EPS = 1e-5
HEAD_NORM_EPS = 1e-5
PEER_HEADS = 8
PEER_NKEYS = 128
PEER_TOPK = 16
PEER_HALF = 128
NEG = float(np.finfo(np.float32).min)

STAIR = tuple((r1, PEER_TOPK // (r1 + 1)) for r1 in range(PEER_TOPK))
STAIR_ROWS = sum(n for _, n in STAIR)
STAIR_PAD = -(-STAIR_ROWS // SUBLANES) * SUBLANES


def _dot(a, b):
    return jnp.dot(a, b, preferred_element_type=F32)


def _dot_nt(a, b):
    return lax.dot_general(a, b, (((1,), (1,)), ((), ())), preferred_element_type=F32)


def _layer_norm(x, g, b):
    mu = jnp.mean(x, axis=-1, keepdims=True)
    xc = x - mu
    var = jnp.mean(xc * xc, axis=-1, keepdims=True)
    return xc * lax.rsqrt(var + LN_EPS) * g + b


def _params(sem, vmem=VMEM_LIMIT):
    return pltpu.CompilerParams(dimension_semantics=sem, vmem_limit_bytes=vmem)


def _rope(p, cos, sin):
    lane = lax.broadcasted_iota(jnp.int32, (p.shape[0], LANES), 1)
    first_half = (lane % DF_DIM) < (DF_DIM // 2)
    outs = []
    for s in range(p.shape[1] // LANES):
        slab = p[:, s * LANES:(s + 1) * LANES]
        rot = jnp.where(first_half, pltpu.roll(slab, LANES - DF_DIM // 2, 1),
                        pltpu.roll(slab, DF_DIM // 2, 1))
        outs.append(slab * cos + rot * sin)
    return jnp.concatenate(outs, axis=1)


def _inproj_kernel(x_ref, w_ref, bg_ref, cos_ref, sin_ref,
                   qsb_ref, ksb_ref, vsb_ref, qdf_ref, kdf_ref, vdf_ref,
                   ksb16_ref, vsb16_ref, kdf16_ref, vdf16_ref, ga_ref, gb_ref):
    x = x_ref[...]
    cos = cos_ref[...]
    sin = sin_ref[...]
    w = 512

    def proj(c0, c1):
        return _dot(x, w_ref[:, c0:c1])

    qsb_ref[...] = (proj(0, w) * (SB_DIM ** -0.5)).astype(BF16)
    k = proj(w, 2 * w)
    ksb_ref[...] = k
    ksb16_ref[...] = k.astype(BF16)
    v = proj(2 * w, 3 * w)
    vsb_ref[...] = v
    vsb16_ref[...] = v.astype(BF16)
    qdf_ref[...] = (_rope(proj(3 * w, 4 * w), cos, sin) * (DF_DIM ** -0.5)).astype(BF16)
    k = _rope(proj(4 * w, 5 * w), cos, sin)
    kdf_ref[...] = k
    kdf16_ref[...] = k.astype(BF16)
    v = proj(5 * w, 6 * w)
    vdf_ref[...] = v
    vdf16_ref[...] = v.astype(BF16)
    d = ga_ref.shape[1]
    ga_ref[...] = jax.nn.sigmoid(proj(6 * w, 6 * w + d) + bg_ref[:, :d]).astype(BF16)
    gb_ref[...] = jax.nn.sigmoid(proj(6 * w + d, 6 * w + 2 * d) + bg_ref[:, d:]).astype(BF16)


def _inproj(x16, w16, bg, cos, sin, tm):
    n, d = x16.shape
    nblk = cos.shape[0] // tm
    row = lambda i: (i, 0)
    const = lambda i: (0, 0)
    tab = lambda i: (i % nblk, 0)
    wide = 512
    out_shape = ([jax.ShapeDtypeStruct((n, wide), BF16)] + [jax.ShapeDtypeStruct((n, wide), F32)] * 2
                 + [jax.ShapeDtypeStruct((n, wide), BF16)] + [jax.ShapeDtypeStruct((n, wide), F32)] * 2
                 + [jax.ShapeDtypeStruct((n, wide), BF16)] * 4 + [jax.ShapeDtypeStruct((n, d), BF16)] * 2)
    out_specs = [pl.BlockSpec((tm, wide), row)] * 10 + [pl.BlockSpec((tm, d), row)] * 2
    return pl.pallas_call(
        _inproj_kernel,
        grid=(n // tm,),
        in_specs=[pl.BlockSpec((tm, d), row),
                  pl.BlockSpec(w16.shape, const),
                  pl.BlockSpec(bg.shape, const),
                  pl.BlockSpec((tm, LANES), tab),
                  pl.BlockSpec((tm, LANES), tab)],
        out_specs=out_specs,
        out_shape=out_shape,
        compiler_params=_params(("parallel",)),
        name="inproj",
    )(x16, w16, bg, cos, sin)


def _cumsum_mats(n):
    r = lax.broadcasted_iota(jnp.int32, (n, n), 0)
    c = lax.broadcasted_iota(jnp.int32, (n, n), 1)
    l = (r > c).astype(BF16)
    return jnp.concatenate([l, l], axis=0)


def _sb_scores(z, mask):
    e = jnp.exp(-jnp.abs(z))
    r = 1.0 / (1.0 + e)
    lk = jnp.log(r) - jnp.maximum(z, 0.0)
    sig = jnp.where(z >= 0.0, r, e * r)
    if mask is not None:
        lk = jnp.where(mask, lk, 0.0)
        sig = jnp.where(mask, sig, 0.0)
    return lk, sig


def _sums_to_right(lk, l2):
    hi = lk.astype(BF16)
    lo = (lk - hi.astype(F32)).astype(BF16)
    return _dot(jnp.concatenate([hi, lo], axis=1), l2)


def _sb_weights(z, mask, l2, csum):
    lk, sig = _sb_scores(z, mask)
    cum = _sums_to_right(lk, l2)
    a = sig * jnp.exp(csum + cum)
    return a, csum + cum[:, :1] + lk[:, :1]


def _masked_heads(q, n_heads, dim):
    lane = lax.broadcasted_iota(jnp.int32, (q.shape[0], LANES), 1)
    per = LANES // dim
    out = []
    for h in range(n_heads):
        slab = q[:, (h // per) * LANES:(h // per + 1) * LANES]
        out.append(jnp.where(lane // dim == h % per, slab, 0.0).astype(BF16))
    return out


def _top_group(qi, tq, tk, t):
    nominal = (qi * tq // tk) * tk
    start = pl.multiple_of(jnp.minimum(nominal, t - tk), tq)
    key = start + lax.broadcasted_iota(jnp.int32, (tq, tk), 1)
    pos = qi * tq + lax.broadcasted_iota(jnp.int32, (tq, tk), 0)
    in_group = key >= nominal
    return start, jnp.logical_and(in_group, key < pos), jnp.logical_and(in_group, key <= pos)


def _sb_prefill_kernel(q_ref, k_ref, v_ref, o_ref, *, tq, tk):
    qi = pl.program_id(1)
    t, width = k_ref.shape[1], q_ref.shape[2]
    nh = width // SB_DIM
    pairs = nh // 2
    qms = _masked_heads(q_ref[0].astype(F32), nh, SB_DIM)
    q2 = [jnp.concatenate(qms[2 * p:2 * p + 2], axis=0) for p in range(pairs)]
    l2 = _cumsum_mats(tk)

    def group(start, mask, csums, accs):
        zs = []
        for p in range(pairs):
            z2 = _dot_nt(q2[p], k_ref[0, pl.ds(start, tk), p * LANES:(p + 1) * LANES])
            zs += [z2[:tq], z2[tq:]]
        scores = [_sb_scores(z, mask) for z in zs]
        cums = _sums_to_right(jnp.concatenate([lk for lk, _ in scores], axis=0), l2)
        new_c, a16 = [], []
        for h, (lk, sig) in enumerate(scores):
            cum = cums[h * tq:(h + 1) * tq]
            a16.append((sig * jnp.exp(csums[h] + cum)).astype(BF16))
            new_c.append(csums[h] + cum[:, :1] + lk[:, :1])
        new_a = []
        for p in range(pairs):
            o2 = _dot(jnp.concatenate(a16[2 * p:2 * p + 2], axis=0),
                      v_ref[0, pl.ds(start, tk), p * LANES:(p + 1) * LANES])
            new_a += [accs[2 * p] + o2[:tq], accs[2 * p + 1] + o2[tq:]]
        return tuple(new_c), tuple(new_a)

    zero_c = tuple(jnp.zeros((tq, 1), F32) for _ in range(nh))
    zero_a = tuple(jnp.zeros((tq, LANES), F32) for _ in range(nh))
    start, strict, _ = _top_group(qi, tq, tk, t)
    carry = group(start, strict, zero_c, zero_a)
    below = qi * tq // tk

    def body(j, carry):
        return group(pl.multiple_of((below - 1 - j) * tk, tk), None, *carry)

    _, accs = lax.fori_loop(0, below, body, carry)
    lane = lax.broadcasted_iota(jnp.int32, (tq, LANES), 1)
    o_ref[0] = jnp.concatenate(
        [jnp.where(lane < SB_DIM, accs[2 * p], accs[2 * p + 1]) for p in range(pairs)],
        axis=1).astype(o_ref.dtype)


def _sb_prefill(q16, k16, v16, tq=Q_BLOCK, tk=2 * Q_BLOCK):
    b, t, w = q16.shape
    blk_q = pl.BlockSpec((1, tq, w), lambda bi, qi: (bi, qi, 0))
    blk_kv = pl.BlockSpec((1, t, w), lambda bi, qi: (bi, 0, 0))
    return pl.pallas_call(
        functools.partial(_sb_prefill_kernel, tq=tq, tk=tk),
        grid=(b, t // tq),
        in_specs=[blk_q, blk_kv, blk_kv],
        out_specs=blk_q,
        out_shape=jax.ShapeDtypeStruct((b, t, w), BF16),
        compiler_params=_params(("parallel", "arbitrary")),
        name="sb_prefill",
    )(q16, k16, v16)


def _head_norm(o, g, out_scale):
    ms = jnp.mean(o * o, axis=-1, keepdims=True)
    return o * lax.rsqrt(ms + HEAD_NORM_EPS) * g * out_scale


def _df_prefill_kernel(lam_ref, q_ref, k_ref, v_ref, g_ref, o_ref, *, tq, tk, out_scale):
    qi = pl.program_id(2)
    t, width = k_ref.shape[1], q_ref.shape[2]
    nh = width // LANES
    qms = _masked_heads(q_ref[0].astype(F32), 2 * nh, DF_DIM)
    q2 = [jnp.concatenate(qms[2 * h:2 * h + 2], axis=0) for h in range(nh)]

    def scores(start, mask):
        out = []
        for h in range(nh):
            s = _dot_nt(q2[h], k_ref[0, pl.ds(start, tk), h * LANES:(h + 1) * LANES])
            if mask is not None:
                s = jnp.concatenate([jnp.where(mask, s[:tq], NEG), jnp.where(mask, s[tq:], NEG)], axis=0)
            out.append(s)
        return out

    def values(start, ps):
        return [_dot(ps[h].astype(BF16), v_ref[0, pl.ds(start, tk), h * LANES:(h + 1) * LANES])
                for h in range(nh)]

    start, _, inclusive = _top_group(qi, tq, tk, t)
    ss = scores(start, inclusive)
    ms = [jnp.max(s, axis=1, keepdims=True) for s in ss]
    ps = [jnp.exp(s - m) for s, m in zip(ss, ms)]
    ls = [jnp.sum(p, axis=1, keepdims=True) for p in ps]
    accs = values(start, ps)

    def body(j, carry):
        ms, ls, accs = carry
        start = pl.multiple_of(j * tk, tk)
        ss = scores(start, None)
        nm = [jnp.maximum(m, jnp.max(s, axis=1, keepdims=True)) for s, m in zip(ss, ms)]
        alphas = [jnp.exp(m - n) for m, n in zip(ms, nm)]
        ps = [jnp.exp(s - n) for s, n in zip(ss, nm)]
        nl = [a * l + jnp.sum(p, axis=1, keepdims=True) for a, l, p in zip(alphas, ls, ps)]
        na = [a * acc + o for a, acc, o in zip(alphas, accs, values(start, ps))]
        return tuple(nm), tuple(nl), tuple(na)

    _, ls, accs = lax.fori_loop(0, qi * tq // tk, body, (tuple(ms), tuple(ls), tuple(accs)))
    outs = []
    for h in range(nh):
        o = accs[h] / ls[h]
        outs.append(_head_norm(o[:tq] - lam_ref[0] * o[tq:], g_ref[h:h + 1, :], out_scale))
    o_ref[0] = jnp.concatenate(outs, axis=1).astype(o_ref.dtype)


def _df_prefill(lam, q16, k16, v16, g, out_scale, tq=Q_BLOCK, tk=2 * Q_BLOCK, heads=DF_HEADS):
    b, t, w = q16.shape
    hw = heads * LANES
    blk_q = pl.BlockSpec((1, tq, hw), lambda bi, hg, qi: (bi, qi, hg))
    blk_kv = pl.BlockSpec((1, t, hw), lambda bi, hg, qi: (bi, 0, hg))
    return pl.pallas_call(
        functools.partial(_df_prefill_kernel, tq=tq, tk=tk, out_scale=out_scale),
        grid=(b, w // hw, t // tq),
        in_specs=[pl.BlockSpec(memory_space=pltpu.SMEM), blk_q, blk_kv, blk_kv,
                  pl.BlockSpec((None, heads, LANES), lambda bi, hg, qi: (hg, 0, 0))],
        out_specs=blk_q,
        out_shape=jax.ShapeDtypeStruct((b, t, w), BF16),
        compiler_params=_params(("parallel", "parallel", "arbitrary")),
        name="df_prefill",
    )(lam, q16, k16, v16, g.reshape(w // hw, heads, LANES))


def _decode_kernel(pt_ref, lam_ref, qsb_ref, ksbn_ref, vsbn_ref, qdf_ref, kdfn_ref, vdfn_ref, g_ref,
                   *rest, pg, n_new, out_scale):
    del pt_ref
    sbk = rest[0:pg]
    sbv = rest[pg:2 * pg]
    dfk = rest[2 * pg:3 * pg]
    dfv = rest[3 * pg:4 * pg]
    oa_ref, ob_ref = rest[4 * pg:4 * pg + 2]
    qxa_ref, qxb_ref, csum_ref, acca_ref, m_ref, l_ref, accb_ref = rest[4 * pg + 2:]
    c = pl.program_id(1)
    nrow, width = qxa_ref.shape
    page = sbk[0].shape[1]
    l2 = _cumsum_mats(page)

    @pl.when(c == 0)
    def _():
        lane = lax.broadcasted_iota(jnp.int32, (n_new, width), 1)
        qa = qsb_ref[...].astype(F32)
        qb = qdf_ref[...].astype(F32)
        for h in range(nrow // n_new):
            sel = (lane // (width * n_new // nrow)) == h
            qxa_ref[h * n_new:(h + 1) * n_new, :] = jnp.where(sel, qa, 0.0)
            qxb_ref[h * n_new:(h + 1) * n_new, :] = jnp.where(sel, qb, 0.0)
        pad = jnp.zeros((page - n_new, width), F32)
        tok = lax.broadcasted_iota(jnp.int32, (nrow, page), 0) % n_new
        key = lax.broadcasted_iota(jnp.int32, (nrow, page), 1)
        kb = jnp.concatenate([ksbn_ref[...], pad], axis=0).astype(BF16)
        vb = jnp.concatenate([vsbn_ref[...], pad], axis=0).astype(BF16)
        a, csum = _sb_weights(_dot_nt(qxa_ref[...].astype(BF16), kb), key < tok, l2,
                              jnp.zeros((nrow, 1), F32))
        csum_ref[...] = csum
        acca_ref[...] = _dot(a.astype(BF16), vb)
        kb = jnp.concatenate([kdfn_ref[...], pad], axis=0).astype(BF16)
        vb = jnp.concatenate([vdfn_ref[...], pad], axis=0).astype(BF16)
        s = jnp.where(key <= tok, _dot_nt(qxb_ref[...].astype(BF16), kb), NEG)
        m = jnp.max(s, axis=1, keepdims=True)
        p = jnp.exp(s - m)
        m_ref[...] = m
        l_ref[...] = jnp.sum(p, axis=1, keepdims=True)
        accb_ref[...] = _dot(p.astype(BF16), vb)

    def slots(refs):
        return jnp.concatenate([r[...].astype(BF16) for r in refs], axis=1)

    z = _dot(qxa_ref[...].astype(BF16), slots(sbk))
    lk, sig = _sb_scores(z, None)
    cols = [slice(g * page, (g + 1) * page) for g in range(pg)]
    cums = _sums_to_right(jnp.concatenate([lk[:, cs] for cs in cols], axis=0), l2)
    csum = csum_ref[...]
    a16 = []
    for g, cs in enumerate(cols):
        cum = cums[g * nrow:(g + 1) * nrow]
        a16.append((sig[:, cs] * jnp.exp(csum + cum)).astype(BF16))
        csum = csum + cum[:, :1] + lk[:, cs][:, :1]
    csum_ref[...] = csum
    acca = acca_ref[...] + _dot_nt(jnp.concatenate(a16, axis=1), slots(sbv))
    acca_ref[...] = acca

    s = _dot(qxb_ref[...].astype(BF16), slots(dfk))
    m_old = m_ref[...]
    m = jnp.maximum(m_old, jnp.max(s, axis=1, keepdims=True))
    alpha = jnp.exp(m_old - m)
    p = jnp.exp(s - m)
    l = alpha * l_ref[...] + jnp.sum(p, axis=1, keepdims=True)
    vb = jnp.concatenate(
        [jnp.concatenate([r[pl.ds(h, page, stride=DF_HEADS), :] for h in range(DF_HEADS)],
                         axis=1).astype(BF16) for r in dfv],
        axis=0)
    accb = alpha * accb_ref[...] + _dot(p.astype(BF16), vb)
    m_ref[...] = m
    l_ref[...] = l
    accb_ref[...] = accb

    @pl.when(c == pl.num_programs(1) - 1)
    def _():
        lane = lax.broadcasted_iota(jnp.int32, (n_new, width), 1)
        out = jnp.zeros((n_new, width), F32)
        for h in range(SB_HEADS):
            out = out + jnp.where(lane // SB_DIM == h, acca[h * n_new:(h + 1) * n_new, :], 0.0)
        oa_ref[...] = out.astype(oa_ref.dtype)
        o = accb / l
        for h in range(DF_HEADS):
            lanes = slice(h * LANES, (h + 1) * LANES)
            o0 = o[(2 * h) * n_new:(2 * h + 1) * n_new, lanes]
            o1 = o[(2 * h + 1) * n_new:(2 * h + 2) * n_new, lanes]
            oh = _head_norm(o0 - lam_ref[0] * o1, g_ref[h:h + 1, :], out_scale)
            ob_ref[:, lanes] = oh.astype(ob_ref.dtype)


def _decode_attention(layer, page_table, lam, qsb, ksbn, vsbn, qdf, kdfn, vdfn, g,
                      c_sbk, c_sbv, c_dfk, c_dfv, out_scale, pg=8):
    nb, n_new, width = qsb.shape
    n_pages = page_table.shape[1]
    page = c_sbk.shape[3]
    new_spec = pl.BlockSpec((None, n_new, width), lambda b, c, pt: (b, 0, 0))

    def page_specs(block):
        def spec(g_):
            def imap(b, c, pt):
                return (layer, pt[b, n_pages - 1 - (c * pg + g_)], 0, 0)
            return pl.BlockSpec((None, None) + block, imap)
        return [spec(g_) for g_ in range(pg)]

    t_specs = page_specs((width, page))
    v_specs = page_specs((page * DF_HEADS, LANES))
    nrow = SB_HEADS * n_new
    grid_spec = pltpu.PrefetchScalarGridSpec(
        num_scalar_prefetch=1,
        grid=(nb, n_pages // pg),
        in_specs=[pl.BlockSpec(memory_space=pltpu.SMEM)] + [new_spec] * 6
                 + [pl.BlockSpec(g.shape, lambda b, c, pt: (0, 0))] + t_specs * 3 + v_specs,
        out_specs=[new_spec, new_spec],
        scratch_shapes=[pltpu.VMEM((nrow, width), F32), pltpu.VMEM((nrow, width), F32),
                        pltpu.VMEM((nrow, 1), F32), pltpu.VMEM((nrow, width), F32),
                        pltpu.VMEM((nrow, 1), F32), pltpu.VMEM((nrow, 1), F32),
                        pltpu.VMEM((nrow, width), F32)],
    )
    return pl.pallas_call(
        functools.partial(_decode_kernel, pg=pg, n_new=n_new, out_scale=out_scale),
        grid_spec=grid_spec,
        out_shape=[jax.ShapeDtypeStruct((nb, n_new, width), BF16)] * 2,
        compiler_params=_params(("parallel", "arbitrary")),
        name="decode_attention",
    )(page_table, lam, qsb, ksbn, vsbn, qdf, kdfn, vdfn, g,
      *([c_sbk] * pg), *([c_sbv] * pg), *([c_dfk] * pg), *([c_dfv] * pg))


def _merge_kernel(x_ref, oa_ref, ob_ref, ga_ref, gb_ref, wa_ref, wb_ref, wo_ref, g_ref, b_ref,
                  h_ref, h16_ref, *, alpha):
    merged = (ga_ref[...].astype(F32) * _dot(oa_ref[...], wa_ref[...])
              + gb_ref[...].astype(F32) * _dot(ob_ref[...], wb_ref[...]))
    y = alpha * x_ref[...] + _dot(merged.astype(BF16), wo_ref[...])
    h = _layer_norm(y, g_ref[...], b_ref[...])
    h_ref[...] = h
    h16_ref[...] = h.astype(BF16)


def _merge(x, oa, ob, ga, gb, wa, wb, wo, g, b, alpha, tm):
    n, d = x.shape
    row = lambda i: (i, 0)
    const = lambda i: (0, 0)
    return pl.pallas_call(
        functools.partial(_merge_kernel, alpha=alpha),
        grid=(n // tm,),
        in_specs=[pl.BlockSpec((tm, d), row), pl.BlockSpec((tm, oa.shape[1]), row),
                  pl.BlockSpec((tm, ob.shape[1]), row), pl.BlockSpec((tm, d), row),
                  pl.BlockSpec((tm, d), row), pl.BlockSpec(wa.shape, const),
                  pl.BlockSpec(wb.shape, const), pl.BlockSpec(wo.shape, const),
                  pl.BlockSpec(g.shape, const), pl.BlockSpec(b.shape, const)],
        out_specs=[pl.BlockSpec((tm, d), row)] * 2,
        out_shape=[jax.ShapeDtypeStruct((n, d), F32), jax.ShapeDtypeStruct((n, d), BF16)],
        compiler_params=_params(("parallel",)),
        name="merge_ln",
    )(x, oa, ob, ga, gb, wa, wb, wo, g, b)


def _extract_topk(s, k, val_ref, idx_ref, row0):
    n = s.shape[0]
    iota = lax.broadcasted_iota(jnp.int32, s.shape, 0).astype(F32)
    for r in range(k):
        m = jnp.max(s, axis=0, keepdims=True)
        idx = jnp.min(jnp.where(s == m, iota, float(n)), axis=0, keepdims=True)
        val_ref[pl.ds(row0 + r, 1), :] = m
        idx_ref[pl.ds(row0 + r, 1), :] = idx
        s = jnp.where(iota == idx, -jnp.inf, s)


def _route_kernel(h_ref, wq_ref, keys_ref, i1_ref, i2_ref, g_ref,
                  s_ref, v_ref, i_ref, cand_ref, ci_ref, best_ref, pos_ref, sel_ref):
    tm = h_ref.shape[0]
    nsub = 2 * PEER_HEADS
    q = _dot(h_ref[...], wq_ref[...]).astype(BF16)
    for hc in range(nsub):
        s_ref[hc] = _dot_nt(keys_ref[hc], q[:, hc * PEER_HALF:(hc + 1) * PEER_HALF])

    def level1(hc, _):
        _extract_topk(s_ref[hc], PEER_TOPK, v_ref, i_ref, pl.multiple_of(hc * PEER_TOPK, PEER_TOPK))
        return 0

    lax.fori_loop(0, nsub, level1, 0)

    cand_ref[STAIR_ROWS:, :] = jnp.full((STAIR_PAD - STAIR_ROWS, tm), -jnp.inf, F32)
    ci_ref[STAIR_ROWS:, :] = jnp.zeros((STAIR_PAD - STAIR_ROWS, tm), F32)

    def level2(h, _):
        base1 = pl.multiple_of(2 * h * PEER_TOPK, PEER_TOPK)
        base2 = pl.multiple_of((2 * h + 1) * PEER_TOPK, PEER_TOPK)
        off = 0
        for r1, n2 in STAIR:
            cand_ref[off:off + n2, :] = v_ref[pl.ds(base1 + r1, 1), :] + v_ref[pl.ds(base2, n2), :]
            ci_ref[off:off + n2, :] = (i_ref[pl.ds(base1 + r1, 1), :] * PEER_NKEYS
                                       + i_ref[pl.ds(base2, n2), :])
            off += n2
        _extract_topk(cand_ref[...], PEER_TOPK, best_ref, pos_ref, 0)
        best = best_ref[...]
        e = jnp.exp(best - best[:1, :])
        gate = e / jnp.sum(e, axis=0, keepdims=True)
        ci = ci_ref[...]
        iota = lax.broadcasted_iota(jnp.int32, ci.shape, 0).astype(F32)
        out0 = pl.multiple_of(h * PEER_TOPK, PEER_TOPK)
        for r in range(PEER_TOPK):
            eidx = jnp.max(jnp.where(iota == pos_ref[r:r + 1, :], ci, -1.0), axis=0, keepdims=True)
            first = jnp.floor(eidx * (1.0 / PEER_NKEYS))
            sel_ref[0, pl.ds(out0 + r, 1), :] = first
            sel_ref[1, pl.ds(out0 + r, 1), :] = eidx - first * PEER_NKEYS
        sel_ref[2, pl.ds(out0, PEER_TOPK), :] = gate
        return 0

    lax.fori_loop(0, PEER_HEADS, level2, 0)
    for c0 in range(0, tm, LANES):
        i1_ref[c0:c0 + LANES, :] = sel_ref[0, :, c0:c0 + LANES].T
        i2_ref[c0:c0 + LANES, :] = sel_ref[1, :, c0:c0 + LANES].T
        g_ref[c0:c0 + LANES, :] = sel_ref[2, :, c0:c0 + LANES].T


def _route(h16, wq16, keys16, tm):
    n, d = h16.shape
    nsub = 2 * PEER_HEADS
    nsel = PEER_HEADS * PEER_TOPK
    row = lambda i: (i, 0)
    return pl.pallas_call(
        _route_kernel,
        grid=(n // tm,),
        in_specs=[pl.BlockSpec((tm, d), row), pl.BlockSpec(wq16.shape, lambda i: (0, 0)),
                  pl.BlockSpec(keys16.shape, lambda i: (0, 0, 0))],
        out_specs=[pl.BlockSpec((tm, nsel), row)] * 3,
        out_shape=[jax.ShapeDtypeStruct((n, nsel), F32)] * 3,
        scratch_shapes=[pltpu.VMEM((nsub, PEER_NKEYS, tm), F32),
                        pltpu.VMEM((nsub * PEER_TOPK, tm), F32),
                        pltpu.VMEM((nsub * PEER_TOPK, tm), F32),
                        pltpu.VMEM((STAIR_PAD, tm), F32),
                        pltpu.VMEM((STAIR_PAD, tm), F32),
                        pltpu.VMEM((PEER_TOPK, tm), F32),
                        pltpu.VMEM((PEER_TOPK, tm), F32),
                        pltpu.VMEM((3, nsel, tm), F32)],
        compiler_params=_params(("parallel",)),
        name="peer_route",
    )(h16, wq16, keys16)


TOKENS_PER_STEP = 16
EXPERT_CHUNK = 4 * MXU_DIM


def _gelu(x):
    return 0.5 * x * (1.0 + lax.erf(x * (2.0 ** -0.5)))


def _experts_kernel(h_ref, h16_ref, i1_ref, i2_ref, gate_ref, u_ref, v_ref, lng_ref, lnb_ref,
                    o_ref, o16_ref, c_ref, *, alpha, stride):
    acc_ref = o_ref
    p, j = pl.program_id(1), pl.program_id(2)
    tm = h_ref.shape[0]
    te = v_ref.shape[0]
    keys = c_ref.shape[0] // stride

    @pl.when(jnp.logical_and(p == 0, j == 0))
    def _():
        acc_ref[...] = jnp.zeros_like(acc_ref)

    @pl.when(j == 0)
    def _():
        iota1 = (lax.broadcasted_iota(jnp.int32, (keys, PEER_NKEYS), 0) + p * keys).astype(F32)
        iota2 = lax.broadcasted_iota(jnp.int32, (PEER_NKEYS, PEER_NKEYS), 0).astype(F32)

        def tokens(i, _):
            t0 = i * TOKENS_PER_STEP
            pts, rts = [], []
            for u in range(TOKENS_PER_STEP):
                i1 = i1_ref[pl.ds(t0 + u, 1), :]
                i2 = i2_ref[pl.ds(t0 + u, 1), :]
                gt = gate_ref[pl.ds(t0 + u, 1), :]
                pts.append(jnp.where(iota1 == i1, 1.0, 0.0).astype(BF16))
                rts.append(jnp.where(iota2 == i2, gt, 0.0).astype(BF16))
            tiles = [_dot_nt(pt, rt) for pt, rt in zip(pts, rts)]
            for u, tile in enumerate(tiles):
                c_ref[pl.ds(t0 + u, keys, stride=stride), :] = tile
            return 0

        lax.fori_loop(0, tm // TOKENS_PER_STEP, tokens, 0)

    x = h16_ref[...]
    per = EXPERT_CHUNK // PEER_NKEYS
    y = None
    for r in range(te // EXPERT_CHUNK):
        rows = slice(r * EXPERT_CHUNK, (r + 1) * EXPERT_CHUNK)
        a = _gelu(_dot(x, u_ref[:, rows]))
        first = (j * (te // EXPERT_CHUNK) + r) * per
        c = jnp.concatenate(
            [c_ref[pl.ds(pl.multiple_of((first + s) * stride, SUBLANES), tm), :] for s in range(per)],
            axis=1)
        part = _dot((a * c).astype(BF16), v_ref[rows, :])
        y = part if y is None else y + part
    acc_ref[...] += y

    @pl.when(jnp.logical_and(p == pl.num_programs(1) - 1, j == pl.num_programs(2) - 1))
    def _():
        out = _layer_norm(alpha * h_ref[...] + acc_ref[...], lng_ref[...], lnb_ref[...])
        o_ref[...] = out
        o16_ref[...] = out.astype(BF16)


def _experts(h, h16, i1, i2, gate, u16, v16, lng, lnb, alpha, tm, te, passes):
    n, d = h.shape
    ne = v16.shape[0]
    stride = tm + SUBLANES
    blocks = ne // (te * passes)
    row = lambda i, p, j: (i, 0)
    const = lambda i, p, j: (0, 0)
    blk = lambda i, p, j: (p * blocks + j, 0)
    nsel = i1.shape[1]
    once = functools.partial(pl.BlockSpec, index_map=row, pipeline_mode=pl.Buffered(1))
    return pl.pallas_call(
        functools.partial(_experts_kernel, alpha=alpha, stride=stride),
        grid=(n // tm, passes, blocks),
        in_specs=[once((tm, d)), once((tm, d)), once((tm, nsel)), once((tm, nsel)), once((tm, nsel)),
                  pl.BlockSpec((d, te), lambda i, p, j: (0, p * blocks + j)), pl.BlockSpec((te, d), blk),
                  pl.BlockSpec(lng.shape, const), pl.BlockSpec(lnb.shape, const)],
        out_specs=[pl.BlockSpec((tm, d), row)] * 2,
        out_shape=[jax.ShapeDtypeStruct((n, d), F32), jax.ShapeDtypeStruct((n, d), BF16)],
        scratch_shapes=[pltpu.VMEM((PEER_NKEYS // passes * stride, PEER_NKEYS), F32)],
        compiler_params=_params(("parallel", "arbitrary", "arbitrary")),
        name="peer_experts",
    )(h, h16, i1, i2, gate, u16, v16, lng, lnb)


def _rope_tables(pos):
    half = DF_DIM // 2
    inv = ROPE_THETA ** (-jnp.arange(half, dtype=F32) / half)
    ang = pos[:, None] * inv[None, :]
    cos, sin = jnp.cos(ang), jnp.sin(ang)
    reps = LANES // DF_DIM
    return (jnp.tile(jnp.concatenate([cos, cos], axis=1), (1, reps)),
            jnp.tile(jnp.concatenate([-sin, sin], axis=1), (1, reps)))


def _token_stage(x, x16, attn, lp, tm_proj, tm_route, exp_tiles, cos, sin, alpha):
    proj = _inproj(x16, lp["w_in"], lp["b_gate"], cos, sin, tm_proj)
    oa, ob = attn(proj)
    ga, gb = proj[10], proj[11]
    h, h16 = _merge(x, oa, ob, ga, gb, lp["w_br_a"], lp["w_br_b"], lp["w_o"], lp["ln1_g"], lp["ln1_b"],
                    alpha, tm_proj)
    i1, i2, gate = _route(h16, lp["peer_wq"], lp["peer_keys"], tm_route)
    y, y16 = _experts(h, h16, i1, i2, gate, lp["peer_u"], lp["peer_v"], lp["ln2_g"], lp["ln2_b"],
                      alpha, *exp_tiles)
    return y, y16, proj


def kernel(x_prompt, x_sample, cache_sb_k, cache_sb_v, cache_df_k, cache_df_v, page_table, meta_tokens, w_in, b_gate, w_br_a, w_br_b, w_o, lam_q1, lam_k1, lam_q2, lam_k2, df_norm_g, ln1_g, ln1_b, peer_wq, peer_sub_keys, peer_u, peer_v, ln2_g, ln2_b):
    depth, d = w_in.shape[0], w_in.shape[1]
    batch, seq, _ = x_prompt.shape
    nb, n_new, _ = x_sample.shape
    alpha = (2 * depth) ** 0.25
    t_real = seq + N_META
    t_pad = -(-t_real // Q_BLOCK) * Q_BLOCK
    n_pool, page = cache_sb_k.shape[1], cache_sb_k.shape[2]
    width = SB_HEADS * SB_DIM
    past_len = page_table.shape[1] * page

    meta = jnp.broadcast_to(meta_tokens[None].astype(F32), (batch, N_META, d))
    xp = jnp.concatenate([meta, x_prompt, jnp.zeros((batch, t_pad - t_real, d), F32)], axis=1)
    xp = xp.reshape(batch * t_pad, d)
    xp16 = xp.astype(BF16)
    xs = x_sample.reshape(nb * n_new, d)
    xs16 = xs.astype(BF16)
    cos_p, sin_p = _rope_tables(jnp.arange(t_pad, dtype=F32))
    cos_s, sin_s = _rope_tables(past_len + jnp.arange(n_new, dtype=F32))
    cos_s, sin_s = jnp.tile(cos_s, (nb, 1)), jnp.tile(sin_s, (nb, 1))
    caches = [jnp.moveaxis(c, 2, -1).reshape(depth, n_pool, width, page)
              for c in (cache_sb_k, cache_sb_v, cache_df_k)]
    caches.append(cache_df_v.reshape(depth, n_pool, page * DF_HEADS, 2 * DF_DIM))

    rows_p, rows_s = [], []
    for l in range(depth):
        lp = dict(
            w_in=w_in[l].astype(BF16), b_gate=b_gate[l].reshape(1, -1),
            w_br_a=w_br_a[l].astype(BF16), w_br_b=w_br_b[l].astype(BF16), w_o=w_o[l].astype(BF16),
            ln1_g=ln1_g[l].reshape(1, d), ln1_b=ln1_b[l].reshape(1, d),
            peer_wq=peer_wq[l].astype(BF16),
            peer_keys=peer_sub_keys[l].reshape(2 * PEER_HEADS, PEER_NKEYS, PEER_HALF).astype(BF16),
            peer_u=peer_u[l].astype(BF16).T, peer_v=peer_v[l].astype(BF16),
            ln2_g=ln2_g[l].reshape(1, d), ln2_b=ln2_b[l].reshape(1, d))
        lam_init = 0.8 - 0.6 * math.exp(-0.3 * l)
        lam = (jnp.exp(jnp.sum(lam_q1[l].astype(F32) * lam_k1[l].astype(F32)))
               - jnp.exp(jnp.sum(lam_q2[l].astype(F32) * lam_k2[l].astype(F32))) + lam_init).reshape(1)
        g_norm = df_norm_g[l].astype(F32)
        out_scale = 1.0 - lam_init

        def attn_prompt(proj):
            r3 = lambda a: a.reshape(batch, t_pad, width)
            oa = _sb_prefill(r3(proj[0]), r3(proj[6]), r3(proj[7]))
            ob = _df_prefill(lam, r3(proj[3]), r3(proj[8]), r3(proj[9]), g_norm, out_scale)
            return oa.reshape(-1, width), ob.reshape(-1, width)

        def attn_sample(proj):
            r3 = lambda a: a.reshape(nb, n_new, width)
            oa, ob = _decode_attention(l, page_table, lam, r3(proj[0]), r3(proj[1]), r3(proj[2]),
                                       r3(proj[3]), r3(proj[4]), r3(proj[5]), g_norm,
                                       *caches, out_scale)
            return oa.reshape(-1, width), ob.reshape(-1, width)

        xp, xp16, proj_p = _token_stage(xp, xp16, attn_prompt, lp, t_pad // 4, 256, (512, 1024, 1),
                                        cos_p, sin_p, alpha)
        xs, xs16, proj_s = _token_stage(xs, xs16, attn_sample, lp, nb * n_new, 256, (256, 2048, 1),
                                        cos_s, sin_s, alpha)
        rows_p.append([proj_p[i].reshape(batch, t_pad, width)[:, :t_real] for i in (1, 2, 4, 5)])
        rows_s.append([proj_s[i].reshape(nb, n_new, width) for i in (1, 2, 4, 5)])

    y_prompt = xp.reshape(batch, t_pad, d)[:, N_META:t_real]
    y_sample = xs.reshape(nb, n_new, d)

    def stack(rows, i, shape):
        return jnp.stack([r[i] for r in rows]).reshape(shape)

    tp, ts = (depth, batch, t_real), (depth, nb, n_new)
    return (y_prompt, y_sample,
            stack(rows_p, 0, tp + (SB_HEADS, SB_DIM)), stack(rows_p, 1, tp + (SB_HEADS, SB_DIM)),
            stack(rows_p, 2, tp + (DF_HEADS, 2, DF_DIM)), stack(rows_p, 3, tp + (DF_HEADS, 2 * DF_DIM)),
            stack(rows_s, 0, ts + (SB_HEADS, SB_DIM)), stack(rows_s, 1, ts + (SB_HEADS, SB_DIM)),
            stack(rows_s, 2, ts + (DF_HEADS, 2, DF_DIM)), stack(rows_s, 3, ts + (DF_HEADS, 2 * DF_DIM)))
```

```python
import functools
import math

import jax
import jax.numpy as jnp
import numpy as np
from jax import lax
from jax.experimental import pallas as pl
from jax.experimental.pallas import tpu as pltpu

F32 = jnp.float32
BF16 = jnp.bfloat16

LANES = 128
SUBLANES = 8
MXU_DIM = 256
VMEM_LIMIT = 56 * 1024 * 1024

N_META = 16
Q_BLOCK = 128
SB_HEADS = 8
SB_DIM = 64
DF_HEADS = 4
DF_DIM = 64
ROPE_THETA = 10000.0
LN_EPS = 1e-5
HEAD_NORM_EPS = 1e-5
PEER_HEADS = 8
PEER_NKEYS = 128
PEER_TOPK = 16
PEER_HALF = 128
NEG = float(np.finfo(np.float32).min)

STAIR = tuple((r1, PEER_TOPK // (r1 + 1)) for r1 in range(PEER_TOPK))
STAIR_ROWS = sum(n for _, n in STAIR)
STAIR_PAD = -(-STAIR_ROWS // SUBLANES) * SUBLANES


def _dot(a, b):
    return jnp.dot(a, b, preferred_element_type=F32)


def _dot_nt(a, b):
    return lax.dot_general(a, b, (((1,), (1,)), ((), ())), preferred_element_type=F32)


def _layer_norm(x, g, b):
    mu = jnp.mean(x, axis=-1, keepdims=True)
    xc = x - mu
    var = jnp.mean(xc * xc, axis=-1, keepdims=True)
    return xc * lax.rsqrt(var + LN_EPS) * g + b


def _params(sem, vmem=VMEM_LIMIT):
    return pltpu.CompilerParams(dimension_semantics=sem, vmem_limit_bytes=vmem)


def _rope(p, cos, sin):
    lane = lax.broadcasted_iota(jnp.int32, (p.shape[0], LANES), 1)
    first_half = (lane % DF_DIM) < (DF_DIM // 2)
    outs = []
    for s in range(p.shape[1] // LANES):
        slab = p[:, s * LANES:(s + 1) * LANES]
        rot = jnp.where(first_half, pltpu.roll(slab, LANES - DF_DIM // 2, 1),
                        pltpu.roll(slab, DF_DIM // 2, 1))
        outs.append(slab * cos + rot * sin)
    return jnp.concatenate(outs, axis=1)


def _inproj_kernel(x_ref, w_ref, bg_ref, cos_ref, sin_ref,
                   qsb_ref, ksb_ref, vsb_ref, qdf_ref, kdf_ref, vdf_ref,
                   ksb16_ref, vsb16_ref, kdf16_ref, vdf16_ref, ga_ref, gb_ref):
    x = x_ref[...]
    cos = cos_ref[...]
    sin = sin_ref[...]
    w = 512

    def proj(c0, c1):
        return _dot(x, w_ref[:, c0:c1])

    qsb_ref[...] = (proj(0, w) * (SB_DIM ** -0.5)).astype(BF16)
    k = proj(w, 2 * w)
    ksb_ref[...] = k
    ksb16_ref[...] = k.astype(BF16)
    v = proj(2 * w, 3 * w)
    vsb_ref[...] = v
    vsb16_ref[...] = v.astype(BF16)
    qdf_ref[...] = (_rope(proj(3 * w, 4 * w), cos, sin) * (DF_DIM ** -0.5)).astype(BF16)
    k = _rope(proj(4 * w, 5 * w), cos, sin)
    kdf_ref[...] = k
    kdf16_ref[...] = k.astype(BF16)
    v = proj(5 * w, 6 * w)
    vdf_ref[...] = v
    vdf16_ref[...] = v.astype(BF16)
    d = ga_ref.shape[1]
    ga_ref[...] = jax.nn.sigmoid(proj(6 * w, 6 * w + d) + bg_ref[:, :d]).astype(BF16)
    gb_ref[...] = jax.nn.sigmoid(proj(6 * w + d, 6 * w + 2 * d) + bg_ref[:, d:]).astype(BF16)


def _inproj(x16, w16, bg, cos, sin, tm):
    n, d = x16.shape
    assert n % tm == 0 and cos.shape[0] % tm == 0, (n, tm, cos.shape)
    nblk = cos.shape[0] // tm
    row = lambda i: (i, 0)
    const = lambda i: (0, 0)
    tab = lambda i: (i % nblk, 0)
    wide = 512
    out_shape = ([jax.ShapeDtypeStruct((n, wide), BF16)] + [jax.ShapeDtypeStruct((n, wide), F32)] * 2
                 + [jax.ShapeDtypeStruct((n, wide), BF16)] + [jax.ShapeDtypeStruct((n, wide), F32)] * 2
                 + [jax.ShapeDtypeStruct((n, wide), BF16)] * 4 + [jax.ShapeDtypeStruct((n, d), BF16)] * 2)
    out_specs = [pl.BlockSpec((tm, wide), row)] * 10 + [pl.BlockSpec((tm, d), row)] * 2
    return pl.pallas_call(
        _inproj_kernel,
        grid=(n // tm,),
        in_specs=[pl.BlockSpec((tm, d), row),
                  pl.BlockSpec(w16.shape, const),
                  pl.BlockSpec(bg.shape, const),
                  pl.BlockSpec((tm, LANES), tab),
                  pl.BlockSpec((tm, LANES), tab)],
        out_specs=out_specs,
        out_shape=out_shape,
        compiler_params=_params(("parallel",)),
        name="inproj",
    )(x16, w16, bg, cos, sin)


def _cumsum_mats(n):
    r = lax.broadcasted_iota(jnp.int32, (n, n), 0)
    c = lax.broadcasted_iota(jnp.int32, (n, n), 1)
    l = (r > c).astype(BF16)
    return jnp.concatenate([l, l], axis=0)


def _sb_scores(z, mask):
    r = 1.0 / (1.0 + jnp.exp(-jnp.abs(z)))
    lk = jnp.log(r) - jnp.maximum(z, 0.0)
    ls = lk + z
    if mask is not None:
        lk = jnp.where(mask, lk, 0.0)
    return lk, ls


def _sb_attn(ls, between, mask):
    a = jnp.exp(ls + between)
    return a if mask is None else jnp.where(mask, a, 0.0)


def _sums_to_right(lk, l2):
    hi = lk.astype(BF16)
    lo = (lk - hi.astype(F32)).astype(BF16)
    return _dot(jnp.concatenate([hi, lo], axis=1), l2)


def _sb_weights(z, mask, l2, csum):
    lk, ls = _sb_scores(z, mask)
    cum = _sums_to_right(lk, l2)
    return _sb_attn(ls, csum + cum, mask), csum + cum[:, :1] + lk[:, :1]


def _masked_heads(q, n_heads, dim):
    lane = lax.broadcasted_iota(jnp.int32, (q.shape[0], LANES), 1)
    per = LANES // dim
    out = []
    for h in range(n_heads):
        slab = q[:, (h // per) * LANES:(h // per + 1) * LANES]
        out.append(jnp.where(lane // dim == h % per, slab, 0.0).astype(BF16))
    return out


def _top_group(qi, tq, tk, t):
    nominal = (qi * tq // tk) * tk
    start = pl.multiple_of(jnp.minimum(nominal, t - tk), tq)
    key = start + lax.broadcasted_iota(jnp.int32, (tq, tk), 1)
    pos = qi * tq + lax.broadcasted_iota(jnp.int32, (tq, tk), 0)
    in_group = key >= nominal
    return start, jnp.logical_and(in_group, key < pos), jnp.logical_and(in_group, key <= pos)


def _sb_prefill_kernel(q_ref, k_ref, v_ref, o_ref, *, tq, tk):
    qi = pl.program_id(1)
    t, width = k_ref.shape[1], q_ref.shape[2]
    nh = width // SB_DIM
    pairs = nh // 2
    qms = _masked_heads(q_ref[0].astype(F32), nh, SB_DIM)
    q2 = [jnp.concatenate(qms[2 * p:2 * p + 2], axis=0) for p in range(pairs)]
    l2 = _cumsum_mats(tk)

    def group(start, mask, csums, accs):
        zs = []
        for p in range(pairs):
            z2 = _dot_nt(q2[p], k_ref[0, pl.ds(start, tk), p * LANES:(p + 1) * LANES])
            zs += [z2[:tq], z2[tq:]]
        scores = [_sb_scores(z, mask) for z in zs]
        cums = _sums_to_right(jnp.concatenate([lk for lk, _ in scores], axis=0), l2)
        new_c, a16 = [], []
        for h, (lk, ls) in enumerate(scores):
            cum = cums[h * tq:(h + 1) * tq]
            a16.append(_sb_attn(ls, csums[h] + cum, mask).astype(BF16))
            new_c.append(csums[h] + cum[:, :1] + lk[:, :1])
        new_a = []
        for p in range(pairs):
            o2 = _dot(jnp.concatenate(a16[2 * p:2 * p + 2], axis=0),
                      v_ref[0, pl.ds(start, tk), p * LANES:(p + 1) * LANES])
            new_a += [accs[2 * p] + o2[:tq], accs[2 * p + 1] + o2[tq:]]
        return tuple(new_c), tuple(new_a)

    zero_c = tuple(jnp.zeros((tq, 1), F32) for _ in range(nh))
    zero_a = tuple(jnp.zeros((tq, LANES), F32) for _ in range(nh))
    start, strict, _ = _top_group(qi, tq, tk, t)
    carry = group(start, strict, zero_c, zero_a)
    below = qi * tq // tk

    def body(j, carry):
        return group(pl.multiple_of((below - 1 - j) * tk, tk), None, *carry)

    _, accs = lax.fori_loop(0, below, body, carry)
    lane = lax.broadcasted_iota(jnp.int32, (tq, LANES), 1)
    o_ref[0] = jnp.concatenate(
        [jnp.where(lane < SB_DIM, accs[2 * p], accs[2 * p + 1]) for p in range(pairs)],
        axis=1).astype(o_ref.dtype)


def _sb_prefill(q16, k16, v16, tq=Q_BLOCK, tk=2 * Q_BLOCK):
    b, t, w = q16.shape
    blk_q = pl.BlockSpec((1, tq, w), lambda bi, qi: (bi, qi, 0))
    blk_kv = pl.BlockSpec((1, t, w), lambda bi, qi: (bi, 0, 0))
    return pl.pallas_call(
        functools.partial(_sb_prefill_kernel, tq=tq, tk=tk),
        grid=(b, t // tq),
        in_specs=[blk_q, blk_kv, blk_kv],
        out_specs=blk_q,
        out_shape=jax.ShapeDtypeStruct((b, t, w), BF16),
        compiler_params=_params(("parallel", "arbitrary")),
        name="sb_prefill",
    )(q16, k16, v16)


DF_HEAD_BATCH = 2


def _head_norm(o, g, out_scale):
    ms = jnp.mean(o * o, axis=-1, keepdims=True)
    return o * lax.rsqrt(ms + HEAD_NORM_EPS) * g * out_scale


def _df_prefill_kernel(lam_ref, q_ref, k_ref, v_ref, g_ref, o_ref, *, tq, tk, out_scale):
    qi = pl.program_id(2)
    t, width = k_ref.shape[1], q_ref.shape[2]
    nh = width // LANES
    qms = _masked_heads(q_ref[0].astype(F32), 2 * nh, DF_DIM)
    q2 = [jnp.concatenate(qms[2 * h:2 * h + 2], axis=0) for h in range(nh)]

    def scores(start, mask, heads):
        out = []
        for h in heads:
            s = _dot_nt(q2[h], k_ref[0, pl.ds(start, tk), h * LANES:(h + 1) * LANES])
            if mask is not None:
                s = jnp.concatenate([jnp.where(mask, s[:tq], NEG), jnp.where(mask, s[tq:], NEG)], axis=0)
            out.append(s)
        return out

    def values(start, ps, heads):
        return [_dot(p.astype(BF16), v_ref[0, pl.ds(start, tk), h * LANES:(h + 1) * LANES])
                for p, h in zip(ps, heads)]

    batches = [list(range(b0, min(b0 + DF_HEAD_BATCH, nh))) for b0 in range(0, nh, DF_HEAD_BATCH)]

    start, _, inclusive = _top_group(qi, tq, tk, t)
    ms, ls, accs = [], [], []
    for heads in batches:
        ss = scores(start, inclusive, heads)
        m_b = [jnp.max(s, axis=1, keepdims=True) for s in ss]
        ps = [jnp.exp(s - m) for s, m in zip(ss, m_b)]
        ms += m_b
        ls += [jnp.sum(p, axis=1, keepdims=True) for p in ps]
        accs += values(start, ps, heads)

    def body(j, carry):
        ms, ls, accs = carry
        start = pl.multiple_of(j * tk, tk)
        nm, nl, na = [], [], []
        for heads in batches:
            ss = scores(start, None, heads)
            n_b = [jnp.maximum(ms[h], jnp.max(s, axis=1, keepdims=True)) for s, h in zip(ss, heads)]
            alphas = [jnp.exp(ms[h] - n) for n, h in zip(n_b, heads)]
            ps = [jnp.exp(s - n) for s, n in zip(ss, n_b)]
            nm += n_b
            nl += [a * ls[h] + jnp.sum(p, axis=1, keepdims=True) for a, p, h in zip(alphas, ps, heads)]
            na += [a * accs[h] + o for a, o, h in zip(alphas, values(start, ps, heads), heads)]
        return tuple(nm), tuple(nl), tuple(na)

    _, ls, accs = lax.fori_loop(0, qi * tq // tk, body, (tuple(ms), tuple(ls), tuple(accs)))
    outs = []
    for h in range(nh):
        o = accs[h] / ls[h]
        outs.append(_head_norm(o[:tq] - lam_ref[0] * o[tq:], g_ref[h:h + 1, :], out_scale))
    o_ref[0] = jnp.concatenate(outs, axis=1).astype(o_ref.dtype)


def _df_prefill(lam, q16, k16, v16, g, out_scale, tq=Q_BLOCK, tk=4 * Q_BLOCK, heads=DF_HEADS):
    b, t, w = q16.shape
    hw = heads * LANES
    blk_q = pl.BlockSpec((1, tq, hw), lambda bi, hg, qi: (bi, qi, hg))
    blk_kv = pl.BlockSpec((1, t, hw), lambda bi, hg, qi: (bi, 0, hg))
    return pl.pallas_call(
        functools.partial(_df_prefill_kernel, tq=tq, tk=tk, out_scale=out_scale),
        grid=(b, w // hw, t // tq),
        in_specs=[pl.BlockSpec(memory_space=pltpu.SMEM), blk_q, blk_kv, blk_kv,
                  pl.BlockSpec((None, heads, LANES), lambda bi, hg, qi: (hg, 0, 0))],
        out_specs=blk_q,
        out_shape=jax.ShapeDtypeStruct((b, t, w), BF16),
        compiler_params=_params(("parallel", "parallel", "arbitrary")),
        name="df_prefill",
    )(lam, q16, k16, v16, g.reshape(w // hw, heads, LANES))


def _decode_kernel(pt_ref, lam_ref, qsb_ref, ksbn_ref, vsbn_ref, qdf_ref, kdfn_ref, vdfn_ref, g_ref,
                   *rest, pg, n_new, out_scale):
    del pt_ref
    sbk = rest[0:pg]
    sbv = rest[pg:2 * pg]
    dfk = rest[2 * pg:3 * pg]
    dfv = rest[3 * pg:4 * pg]
    oa_ref, ob_ref = rest[4 * pg:4 * pg + 2]
    qxa_ref, qxb_ref, csum_ref, acca_ref, m_ref, l_ref, accb_ref = rest[4 * pg + 2:]
    c = pl.program_id(1)
    nrow, width = qxa_ref.shape
    page = sbk[0].shape[1]
    l2 = _cumsum_mats(page)

    @pl.when(c == 0)
    def _():
        lane = lax.broadcasted_iota(jnp.int32, (n_new, width), 1)
        qa = qsb_ref[...].astype(F32)
        qb = qdf_ref[...].astype(F32)
        for h in range(nrow // n_new):
            sel = (lane // (width * n_new // nrow)) == h
            qxa_ref[h * n_new:(h + 1) * n_new, :] = jnp.where(sel, qa, 0.0)
            qxb_ref[h * n_new:(h + 1) * n_new, :] = jnp.where(sel, qb, 0.0)
        pad = jnp.zeros((page - n_new, width), F32)
        tok = lax.broadcasted_iota(jnp.int32, (nrow, page), 0) % n_new
        key = lax.broadcasted_iota(jnp.int32, (nrow, page), 1)
        kb = jnp.concatenate([ksbn_ref[...], pad], axis=0).astype(BF16)
        vb = jnp.concatenate([vsbn_ref[...], pad], axis=0).astype(BF16)
        a, csum = _sb_weights(_dot_nt(qxa_ref[...].astype(BF16), kb), key < tok, l2,
                              jnp.zeros((nrow, 1), F32))
        csum_ref[...] = csum
        acca_ref[...] = _dot(a.astype(BF16), vb)
        kb = jnp.concatenate([kdfn_ref[...], pad], axis=0).astype(BF16)
        vb = jnp.concatenate([vdfn_ref[...], pad], axis=0).astype(BF16)
        s = jnp.where(key <= tok, _dot_nt(qxb_ref[...].astype(BF16), kb), NEG)
        m = jnp.max(s, axis=1, keepdims=True)
        p = jnp.exp(s - m)
        m_ref[...] = m
        l_ref[...] = jnp.sum(p, axis=1, keepdims=True)
        accb_ref[...] = _dot(p.astype(BF16), vb)

    def slots(refs):
        return jnp.concatenate([r[...].astype(BF16) for r in refs], axis=1)

    z = _dot(qxa_ref[...].astype(BF16), slots(sbk))
    lk, ls = _sb_scores(z, None)
    cols = [slice(g * page, (g + 1) * page) for g in range(pg)]
    cums = _sums_to_right(jnp.concatenate([lk[:, cs] for cs in cols], axis=0), l2)
    csum = csum_ref[...]
    a16 = []
    for g, cs in enumerate(cols):
        cum = cums[g * nrow:(g + 1) * nrow]
        a16.append(_sb_attn(ls[:, cs], csum + cum, None).astype(BF16))
        csum = csum + cum[:, :1] + lk[:, cs][:, :1]
    csum_ref[...] = csum
    acca = acca_ref[...] + _dot_nt(jnp.concatenate(a16, axis=1), slots(sbv))
    acca_ref[...] = acca

    s = _dot(qxb_ref[...].astype(BF16), slots(dfk))
    m_old = m_ref[...]
    m = jnp.maximum(m_old, jnp.max(s, axis=1, keepdims=True))
    alpha = jnp.exp(m_old - m)
    p = jnp.exp(s - m)
    l = alpha * l_ref[...] + jnp.sum(p, axis=1, keepdims=True)
    vb = jnp.concatenate(
        [jnp.concatenate([r[pl.ds(h, page, stride=DF_HEADS), :] for h in range(DF_HEADS)],
                         axis=1).astype(BF16) for r in dfv],
        axis=0)
    accb = alpha * accb_ref[...] + _dot(p.astype(BF16), vb)
    m_ref[...] = m
    l_ref[...] = l
    accb_ref[...] = accb

    @pl.when(c == pl.num_programs(1) - 1)
    def _():
        lane = lax.broadcasted_iota(jnp.int32, (n_new, width), 1)
        out = jnp.zeros((n_new, width), F32)
        for h in range(SB_HEADS):
            out = out + jnp.where(lane // SB_DIM == h, acca[h * n_new:(h + 1) * n_new, :], 0.0)
        oa_ref[...] = out.astype(oa_ref.dtype)
        o = accb / l
        for h in range(DF_HEADS):
            lanes = slice(h * LANES, (h + 1) * LANES)
            o0 = o[(2 * h) * n_new:(2 * h + 1) * n_new, lanes]
            o1 = o[(2 * h + 1) * n_new:(2 * h + 2) * n_new, lanes]
            oh = _head_norm(o0 - lam_ref[0] * o1, g_ref[h:h + 1, :], out_scale)
            ob_ref[:, lanes] = oh.astype(ob_ref.dtype)


def _decode_attention(layer, page_table, lam, qsb, ksbn, vsbn, qdf, kdfn, vdfn, g,
                      c_sbk, c_sbv, c_dfk, c_dfv, out_scale, pg=16):
    nb, n_new, width = qsb.shape
    n_pages = page_table.shape[1]
    page = c_sbk.shape[3]
    assert n_pages % pg == 0, (n_pages, pg)
    new_spec = pl.BlockSpec((None, n_new, width), lambda b, c, pt: (b, 0, 0))

    def page_specs(block):
        def spec(g_):
            def imap(b, c, pt):
                return (layer, pt[b, n_pages - 1 - (c * pg + g_)], 0, 0)
            return pl.BlockSpec((None, None) + block, imap)
        return [spec(g_) for g_ in range(pg)]

    t_specs = page_specs((width, page))
    v_specs = page_specs((page * DF_HEADS, LANES))
    nrow = SB_HEADS * n_new
    grid_spec = pltpu.PrefetchScalarGridSpec(
        num_scalar_prefetch=1,
        grid=(nb, n_pages // pg),
        in_specs=[pl.BlockSpec(memory_space=pltpu.SMEM)] + [new_spec] * 6
                 + [pl.BlockSpec(g.shape, lambda b, c, pt: (0, 0))] + t_specs * 3 + v_specs,
        out_specs=[new_spec, new_spec],
        scratch_shapes=[pltpu.VMEM((nrow, width), F32), pltpu.VMEM((nrow, width), F32),
                        pltpu.VMEM((nrow, 1), F32), pltpu.VMEM((nrow, width), F32),
                        pltpu.VMEM((nrow, 1), F32), pltpu.VMEM((nrow, 1), F32),
                        pltpu.VMEM((nrow, width), F32)],
    )
    return pl.pallas_call(
        functools.partial(_decode_kernel, pg=pg, n_new=n_new, out_scale=out_scale),
        grid_spec=grid_spec,
        out_shape=[jax.ShapeDtypeStruct((nb, n_new, width), BF16)] * 2,
        compiler_params=_params(("parallel", "arbitrary")),
        name="decode_attention",
    )(page_table, lam, qsb, ksbn, vsbn, qdf, kdfn, vdfn, g,
      *([c_sbk] * pg), *([c_sbv] * pg), *([c_dfk] * pg), *([c_dfv] * pg))


def _merge_kernel(x_ref, oa_ref, ob_ref, ga_ref, gb_ref, wa_ref, wb_ref, wo_ref, g_ref, b_ref,
                  h_ref, h16_ref, *, alpha):
    merged = (ga_ref[...].astype(F32) * _dot(oa_ref[...], wa_ref[...])
              + gb_ref[...].astype(F32) * _dot(ob_ref[...], wb_ref[...]))
    y = alpha * x_ref[...] + _dot(merged.astype(BF16), wo_ref[...])
    h = _layer_norm(y, g_ref[...], b_ref[...])
    h_ref[...] = h
    h16_ref[...] = h.astype(BF16)


def _merge(x, oa, ob, ga, gb, wa, wb, wo, g, b, alpha, tm):
    n, d = x.shape
    assert n % tm == 0, (n, tm)
    row = lambda i: (i, 0)
    const = lambda i: (0, 0)
    return pl.pallas_call(
        functools.partial(_merge_kernel, alpha=alpha),
        grid=(n // tm,),
        in_specs=[pl.BlockSpec((tm, d), row), pl.BlockSpec((tm, oa.shape[1]), row),
                  pl.BlockSpec((tm, ob.shape[1]), row), pl.BlockSpec((tm, d), row),
                  pl.BlockSpec((tm, d), row), pl.BlockSpec(wa.shape, const),
                  pl.BlockSpec(wb.shape, const), pl.BlockSpec(wo.shape, const),
                  pl.BlockSpec(g.shape, const), pl.BlockSpec(b.shape, const)],
        out_specs=[pl.BlockSpec((tm, d), row)] * 2,
        out_shape=[jax.ShapeDtypeStruct((n, d), F32), jax.ShapeDtypeStruct((n, d), BF16)],
        compiler_params=_params(("parallel",)),
        name="merge_ln",
    )(x, oa, ob, ga, gb, wa, wb, wo, g, b)


def _extract_topk(s, k, val_ref, idx_ref, row0):
    n = s.shape[0]
    iota = lax.broadcasted_iota(jnp.int32, s.shape, 0).astype(F32)
    for r in range(k):
        m = jnp.max(s, axis=0, keepdims=True)
        idx = jnp.min(jnp.where(s == m, iota, float(n)), axis=0, keepdims=True)
        val_ref[pl.ds(row0 + r, 1), :] = m
        idx_ref[pl.ds(row0 + r, 1), :] = idx
        s = jnp.where(iota == idx, -jnp.inf, s)


def _route_kernel(h_ref, wq_ref, keys_ref, i1_ref, i2_ref, g_ref,
                  s_ref, v_ref, i_ref, cand_ref, ci_ref, best_ref, pos_ref, sel_ref):
    tm = h_ref.shape[0]
    nsub = 2 * PEER_HEADS
    q = _dot(h_ref[...], wq_ref[...]).astype(BF16)
    for hc in range(nsub):
        s_ref[hc] = _dot_nt(keys_ref[hc], q[:, hc * PEER_HALF:(hc + 1) * PEER_HALF])

    def level1(hc, _):
        _extract_topk(s_ref[hc], PEER_TOPK, v_ref, i_ref, pl.multiple_of(hc * PEER_TOPK, PEER_TOPK))
        return 0

    lax.fori_loop(0, nsub, level1, 0)

    cand_ref[STAIR_ROWS:, :] = jnp.full((STAIR_PAD - STAIR_ROWS, tm), -jnp.inf, F32)
    ci_ref[STAIR_ROWS:, :] = jnp.zeros((STAIR_PAD - STAIR_ROWS, tm), F32)

    def level2(h, _):
        base1 = pl.multiple_of(2 * h * PEER_TOPK, PEER_TOPK)
        base2 = pl.multiple_of((2 * h + 1) * PEER_TOPK, PEER_TOPK)
        off = 0
        for r1, n2 in STAIR:
            cand_ref[off:off + n2, :] = v_ref[pl.ds(base1 + r1, 1), :] + v_ref[pl.ds(base2, n2), :]
            ci_ref[off:off + n2, :] = (i_ref[pl.ds(base1 + r1, 1), :] * PEER_NKEYS
                                       + i_ref[pl.ds(base2, n2), :])
            off += n2
        _extract_topk(cand_ref[...], PEER_TOPK, best_ref, pos_ref, 0)
        best = best_ref[...]
        e = jnp.exp(best - best[:1, :])
        gate = e / jnp.sum(e, axis=0, keepdims=True)
        ci = ci_ref[...]
        iota = lax.broadcasted_iota(jnp.int32, ci.shape, 0).astype(F32)
        out0 = pl.multiple_of(h * PEER_TOPK, PEER_TOPK)
        for r in range(PEER_TOPK):
            eidx = jnp.max(jnp.where(iota == pos_ref[r:r + 1, :], ci, -1.0), axis=0, keepdims=True)
            first = jnp.floor(eidx * (1.0 / PEER_NKEYS))
            sel_ref[0, pl.ds(out0 + r, 1), :] = first
            sel_ref[1, pl.ds(out0 + r, 1), :] = eidx - first * PEER_NKEYS
        sel_ref[2, pl.ds(out0, PEER_TOPK), :] = gate
        return 0

    lax.fori_loop(0, PEER_HEADS, level2, 0)
    for c0 in range(0, tm, LANES):
        i1_ref[c0:c0 + LANES, :] = sel_ref[0, :, c0:c0 + LANES].T
        i2_ref[c0:c0 + LANES, :] = sel_ref[1, :, c0:c0 + LANES].T
        g_ref[c0:c0 + LANES, :] = sel_ref[2, :, c0:c0 + LANES].T


def _route(h16, wq16, keys16, tm):
    n, d = h16.shape
    assert n % tm == 0 and tm % LANES == 0, (n, tm)
    nsub = 2 * PEER_HEADS
    nsel = PEER_HEADS * PEER_TOPK
    row = lambda i: (i, 0)
    return pl.pallas_call(
        _route_kernel,
        grid=(n // tm,),
        in_specs=[pl.BlockSpec((tm, d), row), pl.BlockSpec(wq16.shape, lambda i: (0, 0)),
                  pl.BlockSpec(keys16.shape, lambda i: (0, 0, 0))],
        out_specs=[pl.BlockSpec((tm, nsel), row)] * 3,
        out_shape=[jax.ShapeDtypeStruct((n, nsel), F32)] * 3,
        scratch_shapes=[pltpu.VMEM((nsub, PEER_NKEYS, tm), F32),
                        pltpu.VMEM((nsub * PEER_TOPK, tm), F32),
                        pltpu.VMEM((nsub * PEER_TOPK, tm), F32),
                        pltpu.VMEM((STAIR_PAD, tm), F32),
                        pltpu.VMEM((STAIR_PAD, tm), F32),
                        pltpu.VMEM((PEER_TOPK, tm), F32),
                        pltpu.VMEM((PEER_TOPK, tm), F32),
                        pltpu.VMEM((3, nsel, tm), F32)],
        compiler_params=_params(("parallel",)),
        name="peer_route",
    )(h16, wq16, keys16)


TOKENS_PER_STEP = 16
EXPERT_CHUNK = 4 * MXU_DIM


def _gelu(x):
    return 0.5 * x * (1.0 + lax.erf(x * (2.0 ** -0.5)))


def _experts_kernel(h_ref, h16_ref, i1_ref, i2_ref, gate_ref, u_ref, v_ref, lng_ref, lnb_ref,
                    o_ref, o16_ref, c_ref, *, alpha, stride):
    acc_ref = o_ref
    p, j = pl.program_id(1), pl.program_id(2)
    tm = h_ref.shape[0]
    te = v_ref.shape[0]
    keys = c_ref.shape[0] // stride

    @pl.when(jnp.logical_and(p == 0, j == 0))
    def _():
        acc_ref[...] = jnp.zeros_like(acc_ref)

    @pl.when(j == 0)
    def _():
        iota1 = (lax.broadcasted_iota(jnp.int32, (keys, PEER_NKEYS), 0) + p * keys).astype(F32)
        iota2 = lax.broadcasted_iota(jnp.int32, (PEER_NKEYS, PEER_NKEYS), 0).astype(F32)

        def tokens(i, _):
            t0 = i * TOKENS_PER_STEP
            pts, rts = [], []
            for u in range(TOKENS_PER_STEP):
                i1 = i1_ref[pl.ds(t0 + u, 1), :]
                i2 = i2_ref[pl.ds(t0 + u, 1), :]
                gt = gate_ref[pl.ds(t0 + u, 1), :]
                pts.append(jnp.where(iota1 == i1, 1.0, 0.0).astype(BF16))
                rts.append(jnp.where(iota2 == i2, gt, 0.0).astype(BF16))
            tiles = [_dot_nt(pt, rt) for pt, rt in zip(pts, rts)]
            for u, tile in enumerate(tiles):
                c_ref[pl.ds(t0 + u, keys, stride=stride), :] = tile
            return 0

        lax.fori_loop(0, tm // TOKENS_PER_STEP, tokens, 0)

    x = h16_ref[...]
    per = EXPERT_CHUNK // PEER_NKEYS
    y = None
    for r in range(te // EXPERT_CHUNK):
        rows = slice(r * EXPERT_CHUNK, (r + 1) * EXPERT_CHUNK)
        a = _gelu(_dot(x, u_ref[:, rows]))
        first = (j * (te // EXPERT_CHUNK) + r) * per
        c = jnp.concatenate(
            [c_ref[pl.ds(pl.multiple_of((first + s) * stride, SUBLANES), tm), :] for s in range(per)],
            axis=1)
        part = _dot((a * c).astype(BF16), v_ref[rows, :])
        y = part if y is None else y + part
    acc_ref[...] += y

    @pl.when(jnp.logical_and(p == pl.num_programs(1) - 1, j == pl.num_programs(2) - 1))
    def _():
        out = _layer_norm(alpha * h_ref[...] + acc_ref[...], lng_ref[...], lnb_ref[...])
        o_ref[...] = out
        o16_ref[...] = out.astype(BF16)


def _experts(h, h16, i1, i2, gate, u16, v16, lng, lnb, alpha, tm, te, passes):
    n, d = h.shape
    assert n % tm == 0 and tm % TOKENS_PER_STEP == 0 and te % EXPERT_CHUNK == 0, (n, tm, te)
    assert v16.shape[0] % (te * passes) == 0 and PEER_NKEYS % passes == 0, (v16.shape, te, passes)
    ne = v16.shape[0]
    stride = tm + SUBLANES
    blocks = ne // (te * passes)
    row = lambda i, p, j: (i, 0)
    const = lambda i, p, j: (0, 0)
    blk = lambda i, p, j: (p * blocks + j, 0)
    nsel = i1.shape[1]
    once = functools.partial(pl.BlockSpec, index_map=row, pipeline_mode=pl.Buffered(1))
    return pl.pallas_call(
        functools.partial(_experts_kernel, alpha=alpha, stride=stride),
        grid=(n // tm, passes, blocks),
        in_specs=[once((tm, d)), once((tm, d)), once((tm, nsel)), once((tm, nsel)), once((tm, nsel)),
                  pl.BlockSpec((d, te), lambda i, p, j: (0, p * blocks + j)), pl.BlockSpec((te, d), blk),
                  pl.BlockSpec(lng.shape, const), pl.BlockSpec(lnb.shape, const)],
        out_specs=[pl.BlockSpec((tm, d), row)] * 2,
        out_shape=[jax.ShapeDtypeStruct((n, d), F32), jax.ShapeDtypeStruct((n, d), BF16)],
        scratch_shapes=[pltpu.VMEM((PEER_NKEYS // passes * stride, PEER_NKEYS), F32)],
        compiler_params=_params(("parallel", "arbitrary", "arbitrary")),
        name="peer_experts",
    )(h, h16, i1, i2, gate, u16, v16, lng, lnb)


def _rope_tables(pos):
    half = DF_DIM // 2
    inv = ROPE_THETA ** (-jnp.arange(half, dtype=F32) / half)
    ang = pos[:, None] * inv[None, :]
    cos, sin = jnp.cos(ang), jnp.sin(ang)
    reps = LANES // DF_DIM
    return (jnp.tile(jnp.concatenate([cos, cos], axis=1), (1, reps)),
            jnp.tile(jnp.concatenate([-sin, sin], axis=1), (1, reps)))


def _token_stage(x, x16, attn, lp, tm_proj, tm_route, exp_tiles, cos, sin, alpha):
    proj = _inproj(x16, lp["w_in"], lp["b_gate"], cos, sin, tm_proj)
    oa, ob = attn(proj)
    ga, gb = proj[10], proj[11]
    h, h16 = _merge(x, oa, ob, ga, gb, lp["w_br_a"], lp["w_br_b"], lp["w_o"], lp["ln1_g"], lp["ln1_b"],
                    alpha, tm_proj)
    i1, i2, gate = _route(h16, lp["peer_wq"], lp["peer_keys"], tm_route)
    y, y16 = _experts(h, h16, i1, i2, gate, lp["peer_u"], lp["peer_v"], lp["ln2_g"], lp["ln2_b"],
                      alpha, *exp_tiles)
    return y, y16, proj


def kernel(x_prompt, x_sample, cache_sb_k, cache_sb_v, cache_df_k, cache_df_v, page_table, meta_tokens, w_in, b_gate, w_br_a, w_br_b, w_o, lam_q1, lam_k1, lam_q2, lam_k2, df_norm_g, ln1_g, ln1_b, peer_wq, peer_sub_keys, peer_u, peer_v, ln2_g, ln2_b):
    depth, d = w_in.shape[0], w_in.shape[1]
    batch, seq, _ = x_prompt.shape
    nb, n_new, _ = x_sample.shape
    alpha = (2 * depth) ** 0.25
    t_real = seq + N_META
    t_pad = -(-t_real // Q_BLOCK) * Q_BLOCK
    n_pool, page = cache_sb_k.shape[1], cache_sb_k.shape[2]
    width = SB_HEADS * SB_DIM
    past_len = page_table.shape[1] * page

    meta = jnp.broadcast_to(meta_tokens[None].astype(F32), (batch, N_META, d))
    xp = jnp.concatenate([meta, x_prompt, jnp.zeros((batch, t_pad - t_real, d), F32)], axis=1)
    xp = xp.reshape(batch * t_pad, d)
    xp16 = xp.astype(BF16)
    xs = x_sample.reshape(nb * n_new, d)
    xs16 = xs.astype(BF16)
    cos_p, sin_p = _rope_tables(jnp.arange(t_pad, dtype=F32))
    cos_s, sin_s = _rope_tables(past_len + jnp.arange(n_new, dtype=F32))
    cos_s, sin_s = jnp.tile(cos_s, (nb, 1)), jnp.tile(sin_s, (nb, 1))
    caches = [jnp.moveaxis(c, 2, -1).reshape(depth, n_pool, width, page)
              for c in (cache_sb_k, cache_sb_v, cache_df_k)]
    caches.append(cache_df_v.reshape(depth, n_pool, page * DF_HEADS, 2 * DF_DIM))

    rows_p, rows_s = [], []
    for l in range(depth):
        lp = dict(
            w_in=w_in[l].astype(BF16), b_gate=b_gate[l].reshape(1, -1),
            w_br_a=w_br_a[l].astype(BF16), w_br_b=w_br_b[l].astype(BF16), w_o=w_o[l].astype(BF16),
            ln1_g=ln1_g[l].reshape(1, d), ln1_b=ln1_b[l].reshape(1, d),
            peer_wq=peer_wq[l].astype(BF16),
            peer_keys=peer_sub_keys[l].reshape(2 * PEER_HEADS, PEER_NKEYS, PEER_HALF).astype(BF16),
            peer_u=peer_u[l].astype(BF16).T, peer_v=peer_v[l].astype(BF16),
            ln2_g=ln2_g[l].reshape(1, d), ln2_b=ln2_b[l].reshape(1, d))
        lam_init = 0.8 - 0.6 * math.exp(-0.3 * l)
        lam = (jnp.exp(jnp.sum(lam_q1[l].astype(F32) * lam_k1[l].astype(F32)))
               - jnp.exp(jnp.sum(lam_q2[l].astype(F32) * lam_k2[l].astype(F32))) + lam_init).reshape(1)
        g_norm = df_norm_g[l].astype(F32)
        out_scale = 1.0 - lam_init

        def attn_prompt(proj):
            r3 = lambda a: a.reshape(batch, t_pad, width)
            oa = _sb_prefill(r3(proj[0]), r3(proj[6]), r3(proj[7]))
            ob = _df_prefill(lam, r3(proj[3]), r3(proj[8]), r3(proj[9]), g_norm, out_scale)
            return oa.reshape(-1, width), ob.reshape(-1, width)

        def attn_sample(proj):
            r3 = lambda a: a.reshape(nb, n_new, width)
            oa, ob = _decode_attention(l, page_table, lam, r3(proj[0]), r3(proj[1]), r3(proj[2]),
                                       r3(proj[3]), r3(proj[4]), r3(proj[5]), g_norm,
                                       *caches, out_scale)
            return oa.reshape(-1, width), ob.reshape(-1, width)

        xp, xp16, proj_p = _token_stage(xp, xp16, attn_prompt, lp, t_pad // 4, 1024, (512, 1024, 1),
                                        cos_p, sin_p, alpha)
        xs, xs16, proj_s = _token_stage(xs, xs16, attn_sample, lp, nb * n_new, 256, (256, 2048, 1),
                                        cos_s, sin_s, alpha)
        rows_p.append([proj_p[i].reshape(batch, t_pad, width)[:, :t_real] for i in (1, 2, 4, 5)])
        rows_s.append([proj_s[i].reshape(nb, n_new, width) for i in (1, 2, 4, 5)])

    y_prompt = xp.reshape(batch, t_pad, d)[:, N_META:t_real]
    y_sample = xs.reshape(nb, n_new, d)

    def stack(rows, i, shape):
        return jnp.stack([r[i] for r in rows]).reshape(shape)

    tp, ts = (depth, batch, t_real), (depth, nb, n_new)
    return (y_prompt, y_sample,
            stack(rows_p, 0, tp + (SB_HEADS, SB_DIM)), stack(rows_p, 1, tp + (SB_HEADS, SB_DIM)),
            stack(rows_p, 2, tp + (DF_HEADS, 2, DF_DIM)), stack(rows_p, 3, tp + (DF_HEADS, 2 * DF_DIM)),
            stack(rows_s, 0, ts + (SB_HEADS, SB_DIM)), stack(rows_s, 1, ts + (SB_HEADS, SB_DIM)),
            stack(rows_s, 2, ts + (DF_HEADS, 2, DF_DIM)), stack(rows_s, 3, ts + (DF_HEADS, 2 * DF_DIM)))
```

```python
import functools
import math

import jax
import jax.numpy as jnp
import numpy as np
from jax import lax
from jax.experimental import pallas as pl
from jax.experimental.pallas import tpu as pltpu

F32 = jnp.float32
BF16 = jnp.bfloat16

LANES = 128
SUBLANES = 8
MXU_DIM = 256
VMEM_LIMIT = 56 * 1024 * 1024

N_META = 16
Q_BLOCK = 128
SB_HEADS = 8
SB_DIM = 64
DF_HEADS = 4
DF_DIM = 64
ROPE_THETA = 10000.0
LN_EPS = 1e-5
HEAD_NORM_EPS = 1e-5
PEER_HEADS = 8
PEER_NKEYS = 128
PEER_TOPK = 16
PEER_HALF = 128
NEG = float(np.finfo(np.float32).min)

STAIR = tuple((r1, PEER_TOPK // (r1 + 1)) for r1 in range(PEER_TOPK))
STAIR_ROWS = sum(n for _, n in STAIR)
STAIR_PAD = -(-STAIR_ROWS // SUBLANES) * SUBLANES


def _dot(a, b):
    return jnp.dot(a, b, preferred_element_type=F32)


def _dot_nt(a, b):
    return lax.dot_general(a, b, (((1,), (1,)), ((), ())), preferred_element_type=F32)


def _layer_norm(x, g, b):
    mu = jnp.mean(x, axis=-1, keepdims=True)
    xc = x - mu
    var = jnp.mean(xc * xc, axis=-1, keepdims=True)
    return xc * lax.rsqrt(var + LN_EPS) * g + b


def _params(sem, vmem=VMEM_LIMIT):
    return pltpu.CompilerParams(dimension_semantics=sem, vmem_limit_bytes=vmem)


def _rope(p, cos, sin):
    lane = lax.broadcasted_iota(jnp.int32, (p.shape[0], LANES), 1)
    first_half = (lane % DF_DIM) < (DF_DIM // 2)
    outs = []
    for s in range(p.shape[1] // LANES):
        slab = p[:, s * LANES:(s + 1) * LANES]
        rot = jnp.where(first_half, pltpu.roll(slab, LANES - DF_DIM // 2, 1),
                        pltpu.roll(slab, DF_DIM // 2, 1))
        outs.append(slab * cos + rot * sin)
    return jnp.concatenate(outs, axis=1)


def _inproj_kernel(x_ref, w_ref, bg_ref, cos_ref, sin_ref,
                   qsb_ref, ksb_ref, vsb_ref, qdf_ref, kdf_ref, vdf_ref,
                   ksb16_ref, vsb16_ref, kdf16_ref, vdf16_ref, ga_ref, gb_ref):
    x = x_ref[...]
    cos = cos_ref[...]
    sin = sin_ref[...]
    w = 512

    def proj(c0, c1):
        return _dot(x, w_ref[:, c0:c1])

    qsb_ref[...] = (proj(0, w) * (SB_DIM ** -0.5)).astype(BF16)
    k = proj(w, 2 * w)
    ksb_ref[...] = k
    ksb16_ref[...] = k.astype(BF16)
    v = proj(2 * w, 3 * w)
    vsb_ref[...] = v
    vsb16_ref[...] = v.astype(BF16)
    qdf_ref[...] = (_rope(proj(3 * w, 4 * w), cos, sin) * (DF_DIM ** -0.5)).astype(BF16)
    k = _rope(proj(4 * w, 5 * w), cos, sin)
    kdf_ref[...] = k
    kdf16_ref[...] = k.astype(BF16)
    v = proj(5 * w, 6 * w)
    vdf_ref[...] = v
    vdf16_ref[...] = v.astype(BF16)
    d = ga_ref.shape[1]
    ga_ref[...] = jax.nn.sigmoid(proj(6 * w, 6 * w + d) + bg_ref[:, :d]).astype(BF16)
    gb_ref[...] = jax.nn.sigmoid(proj(6 * w + d, 6 * w + 2 * d) + bg_ref[:, d:]).astype(BF16)


def _inproj(x16, w16, bg, cos, sin, tm, t_real):
    n, d = x16.shape
    t_pad = cos.shape[0]
    assert n % t_pad == 0 and t_pad % tm == 0 and t_real <= t_pad, (n, tm, t_pad, t_real)
    nblk = t_pad // tm
    row = lambda i: (i, 0)
    const = lambda i: (0, 0)
    tab = lambda i: (i % nblk, 0)
    wide = 512
    kv_shape = jax.ShapeDtypeStruct((n // t_pad, t_real, wide), F32)
    kv_spec = pl.BlockSpec((None, tm, wide), lambda i: (i // nblk, i % nblk, 0))
    flat16 = jax.ShapeDtypeStruct((n, wide), BF16)
    flat_spec = pl.BlockSpec((tm, wide), row)
    out_shape = ([flat16, kv_shape, kv_shape, flat16, kv_shape, kv_shape] + [flat16] * 4
                 + [jax.ShapeDtypeStruct((n, d), BF16)] * 2)
    out_specs = ([flat_spec, kv_spec, kv_spec, flat_spec, kv_spec, kv_spec] + [flat_spec] * 4
                 + [pl.BlockSpec((tm, d), row)] * 2)
    return pl.pallas_call(
        _inproj_kernel,
        grid=(n // tm,),
        in_specs=[pl.BlockSpec((tm, d), row),
                  pl.BlockSpec(w16.shape, const),
                  pl.BlockSpec(bg.shape, const),
                  pl.BlockSpec((tm, LANES), tab),
                  pl.BlockSpec((tm, LANES), tab)],
        out_specs=out_specs,
        out_shape=out_shape,
        compiler_params=_params(("parallel",)),
        name="inproj",
    )(x16, w16, bg, cos, sin)


def _cumsum_mats(n):
    r = lax.broadcasted_iota(jnp.int32, (n, n), 0)
    c = lax.broadcasted_iota(jnp.int32, (n, n), 1)
    l = (r > c).astype(BF16)
    return jnp.concatenate([l, l], axis=0)


def _sb_scores(z, mask):
    r = 1.0 / (1.0 + jnp.exp(-jnp.abs(z)))
    lk = jnp.log(r) - jnp.maximum(z, 0.0)
    ls = lk + z
    if mask is not None:
        lk = jnp.where(mask, lk, 0.0)
    return lk, ls


def _sb_attn(ls, between, mask):
    a = jnp.exp(ls + between)
    return a if mask is None else jnp.where(mask, a, 0.0)


def _sums_to_right(lk, l2):
    hi = lk.astype(BF16)
    lo = (lk - hi.astype(F32)).astype(BF16)
    return _dot(jnp.concatenate([hi, lo], axis=1), l2)


def _sb_weights(z, mask, l2, csum):
    lk, ls = _sb_scores(z, mask)
    cum = _sums_to_right(lk, l2)
    return _sb_attn(ls, csum + cum, mask), csum + cum[:, :1] + lk[:, :1]


def _masked_heads(q, n_heads, dim):
    lane = lax.broadcasted_iota(jnp.int32, (q.shape[0], LANES), 1)
    per = LANES // dim
    out = []
    for h in range(n_heads):
        slab = q[:, (h // per) * LANES:(h // per + 1) * LANES]
        out.append(jnp.where(lane // dim == h % per, slab, 0.0).astype(BF16))
    return out


def _top_group(qi, tq, tk, t):
    nominal = (qi * tq // tk) * tk
    start = pl.multiple_of(jnp.minimum(nominal, t - tk), tq)
    key = start + lax.broadcasted_iota(jnp.int32, (tq, tk), 1)
    pos = qi * tq + lax.broadcasted_iota(jnp.int32, (tq, tk), 0)
    in_group = key >= nominal
    return start, jnp.logical_and(in_group, key < pos), jnp.logical_and(in_group, key <= pos)


SB_PAIR_BATCH = 4


def _sb_prefill_kernel(q_ref, k_ref, v_ref, o_ref, *, tq, tk):
    qi = pl.program_id(1)
    t, width = k_ref.shape[1], q_ref.shape[2]
    nh = width // SB_DIM
    pairs = nh // 2
    qms = _masked_heads(q_ref[0].astype(F32), nh, SB_DIM)
    q2 = [jnp.concatenate(qms[2 * p:2 * p + 2], axis=0) for p in range(pairs)]
    l2 = _cumsum_mats(tk)

    def group(start, mask, csums, accs):
        new_c, new_a = [], []
        for p0 in range(0, pairs, SB_PAIR_BATCH):
            batch = range(p0, min(p0 + SB_PAIR_BATCH, pairs))
            zs = []
            for p in batch:
                z2 = _dot_nt(q2[p], k_ref[0, pl.ds(start, tk), p * LANES:(p + 1) * LANES])
                zs += [z2[:tq], z2[tq:]]
            scores = [_sb_scores(z, mask) for z in zs]
            cums = _sums_to_right(jnp.concatenate([lk for lk, _ in scores], axis=0), l2)
            a16 = []
            for i, (lk, ls) in enumerate(scores):
                h = 2 * p0 + i
                cum = cums[i * tq:(i + 1) * tq]
                a16.append(_sb_attn(ls, csums[h] + cum, mask).astype(BF16))
                new_c.append(csums[h] + cum[:, :1] + lk[:, :1])
            for i, p in enumerate(batch):
                o2 = _dot(jnp.concatenate(a16[2 * i:2 * i + 2], axis=0),
                          v_ref[0, pl.ds(start, tk), p * LANES:(p + 1) * LANES])
                new_a += [accs[2 * p] + o2[:tq], accs[2 * p + 1] + o2[tq:]]
        return tuple(new_c), tuple(new_a)

    zero_c = tuple(jnp.zeros((tq, 1), F32) for _ in range(nh))
    zero_a = tuple(jnp.zeros((tq, LANES), F32) for _ in range(nh))
    start, strict, _ = _top_group(qi, tq, tk, t)
    carry = group(start, strict, zero_c, zero_a)
    below = qi * tq // tk

    def body(j, carry):
        return group(pl.multiple_of((below - 1 - j) * tk, tk), None, *carry)

    _, accs = lax.fori_loop(0, below, body, carry)
    lane = lax.broadcasted_iota(jnp.int32, (tq, LANES), 1)
    o_ref[0] = jnp.concatenate(
        [jnp.where(lane < SB_DIM, accs[2 * p], accs[2 * p + 1]) for p in range(pairs)],
        axis=1).astype(o_ref.dtype)


def _sb_prefill(q16, k16, v16, tq=Q_BLOCK, tk=2 * Q_BLOCK):
    b, t, w = q16.shape
    blk_q = pl.BlockSpec((1, tq, w), lambda bi, qi: (bi, qi, 0))
    blk_kv = pl.BlockSpec((1, t, w), lambda bi, qi: (bi, 0, 0))
    return pl.pallas_call(
        functools.partial(_sb_prefill_kernel, tq=tq, tk=tk),
        grid=(b, t // tq),
        in_specs=[blk_q, blk_kv, blk_kv],
        out_specs=blk_q,
        out_shape=jax.ShapeDtypeStruct((b, t, w), BF16),
        compiler_params=_params(("parallel", "arbitrary")),
        name="sb_prefill",
    )(q16, k16, v16)


DF_HEAD_BATCH = 2


def _head_norm(o, g, out_scale):
    ms = jnp.mean(o * o, axis=-1, keepdims=True)
    return o * lax.rsqrt(ms + HEAD_NORM_EPS) * g * out_scale


def _df_prefill_kernel(lam_ref, q_ref, k_ref, v_ref, g_ref, o_ref, *, tq, tk, out_scale):
    qi = pl.program_id(2)
    t, width = k_ref.shape[1], q_ref.shape[2]
    nh = width // LANES
    qms = _masked_heads(q_ref[0].astype(F32), 2 * nh, DF_DIM)
    q2 = [jnp.concatenate(qms[2 * h:2 * h + 2], axis=0) for h in range(nh)]

    def scores(start, mask, heads):
        out = []
        for h in heads:
            s = _dot_nt(q2[h], k_ref[0, pl.ds(start, tk), h * LANES:(h + 1) * LANES])
            if mask is not None:
                s = jnp.concatenate([jnp.where(mask, s[:tq], NEG), jnp.where(mask, s[tq:], NEG)], axis=0)
            out.append(s)
        return out

    def values(start, ps, heads):
        return [_dot(p.astype(BF16), v_ref[0, pl.ds(start, tk), h * LANES:(h + 1) * LANES])
                for p, h in zip(ps, heads)]

    batches = [list(range(b0, min(b0 + DF_HEAD_BATCH, nh))) for b0 in range(0, nh, DF_HEAD_BATCH)]

    start, _, inclusive = _top_group(qi, tq, tk, t)
    ms, ls, accs = [], [], []
    for heads in batches:
        ss = scores(start, inclusive, heads)
        m_b = [jnp.max(s, axis=1, keepdims=True) for s in ss]
        ps = [jnp.exp(s - m) for s, m in zip(ss, m_b)]
        ms += m_b
        ls += [jnp.sum(p, axis=1, keepdims=True) for p in ps]
        accs += values(start, ps, heads)

    def body(j, carry):
        ms, ls, accs = carry
        start = pl.multiple_of(j * tk, tk)
        nm, nl, na = [], [], []
        for heads in batches:
            ss = scores(start, None, heads)
            n_b = [jnp.maximum(ms[h], jnp.max(s, axis=1, keepdims=True)) for s, h in zip(ss, heads)]
            alphas = [jnp.exp(ms[h] - n) for n, h in zip(n_b, heads)]
            ps = [jnp.exp(s - n) for s, n in zip(ss, n_b)]
            nm += n_b
            nl += [a * ls[h] + jnp.sum(p, axis=1, keepdims=True) for a, p, h in zip(alphas, ps, heads)]
            na += [a * accs[h] + o for a, o, h in zip(alphas, values(start, ps, heads), heads)]
        return tuple(nm), tuple(nl), tuple(na)

    _, ls, accs = lax.fori_loop(0, qi * tq // tk, body, (tuple(ms), tuple(ls), tuple(accs)))
    outs = []
    for h in range(nh):
        o = accs[h] / ls[h]
        outs.append(_head_norm(o[:tq] - lam_ref[0] * o[tq:], g_ref[h:h + 1, :], out_scale))
    o_ref[0] = jnp.concatenate(outs, axis=1).astype(o_ref.dtype)


def _df_prefill(lam, q16, k16, v16, g, out_scale, tq=Q_BLOCK, tk=4 * Q_BLOCK, heads=DF_HEADS):
    b, t, w = q16.shape
    hw = heads * LANES
    blk_q = pl.BlockSpec((1, tq, hw), lambda bi, hg, qi: (bi, qi, hg))
    blk_kv = pl.BlockSpec((1, t, hw), lambda bi, hg, qi: (bi, 0, hg))
    return pl.pallas_call(
        functools.partial(_df_prefill_kernel, tq=tq, tk=tk, out_scale=out_scale),
        grid=(b, w // hw, t // tq),
        in_specs=[pl.BlockSpec(memory_space=pltpu.SMEM), blk_q, blk_kv, blk_kv,
                  pl.BlockSpec((None, heads, LANES), lambda bi, hg, qi: (hg, 0, 0))],
        out_specs=blk_q,
        out_shape=jax.ShapeDtypeStruct((b, t, w), BF16),
        compiler_params=_params(("parallel", "parallel", "arbitrary")),
        name="df_prefill",
    )(lam, q16, k16, v16, g.reshape(w // hw, heads, LANES))


def _decode_kernel(pt_ref, lam_ref, qsb_ref, ksbn_ref, vsbn_ref, qdf_ref, kdfn_ref, vdfn_ref, g_ref,
                   *rest, pg, n_new, out_scale):
    del pt_ref
    sbk = rest[0:pg]
    sbv = rest[pg:2 * pg]
    dfk = rest[2 * pg:3 * pg]
    dfv = rest[3 * pg:4 * pg]
    oa_ref, ob_ref = rest[4 * pg:4 * pg + 2]
    qxa_ref, qxb_ref, csum_ref, acca_ref, m_ref, l_ref, accb_ref = rest[4 * pg + 2:]
    c = pl.program_id(1)
    nrow, width = qxa_ref.shape
    page = sbk[0].shape[1]
    l2 = _cumsum_mats(page)

    @pl.when(c == 0)
    def _():
        lane = lax.broadcasted_iota(jnp.int32, (n_new, width), 1)
        qa = qsb_ref[...].astype(F32)
        qb = qdf_ref[...].astype(F32)
        for h in range(nrow // n_new):
            sel = (lane // (width * n_new // nrow)) == h
            qxa_ref[h * n_new:(h + 1) * n_new, :] = jnp.where(sel, qa, 0.0)
            qxb_ref[h * n_new:(h + 1) * n_new, :] = jnp.where(sel, qb, 0.0)
        pad = jnp.zeros((page - n_new, width), F32)
        tok = lax.broadcasted_iota(jnp.int32, (nrow, page), 0) % n_new
        key = lax.broadcasted_iota(jnp.int32, (nrow, page), 1)
        kb = jnp.concatenate([ksbn_ref[...], pad], axis=0).astype(BF16)
        vb = jnp.concatenate([vsbn_ref[...], pad], axis=0).astype(BF16)
        a, csum = _sb_weights(_dot_nt(qxa_ref[...].astype(BF16), kb), key < tok, l2,
                              jnp.zeros((nrow, 1), F32))
        csum_ref[...] = csum
        acca_ref[...] = _dot(a.astype(BF16), vb)
        kb = jnp.concatenate([kdfn_ref[...], pad], axis=0).astype(BF16)
        vb = jnp.concatenate([vdfn_ref[...], pad], axis=0).astype(BF16)
        s = jnp.where(key <= tok, _dot_nt(qxb_ref[...].astype(BF16), kb), NEG)
        m = jnp.max(s, axis=1, keepdims=True)
        p = jnp.exp(s - m)
        m_ref[...] = m
        l_ref[...] = jnp.sum(p, axis=1, keepdims=True)
        accb_ref[...] = _dot(p.astype(BF16), vb)

    def slots(refs):
        return jnp.concatenate([r[...].astype(BF16) for r in refs], axis=1)

    z = _dot(qxa_ref[...].astype(BF16), slots(sbk))
    lk, ls = _sb_scores(z, None)
    cols = [slice(g * page, (g + 1) * page) for g in range(pg)]
    cums = _sums_to_right(jnp.concatenate([lk[:, cs] for cs in cols], axis=0), l2)
    csum = csum_ref[...]
    a16 = []
    for g, cs in enumerate(cols):
        cum = cums[g * nrow:(g + 1) * nrow]
        a16.append(_sb_attn(ls[:, cs], csum + cum, None).astype(BF16))
        csum = csum + cum[:, :1] + lk[:, cs][:, :1]
    csum_ref[...] = csum
    acca = acca_ref[...] + _dot_nt(jnp.concatenate(a16, axis=1), slots(sbv))
    acca_ref[...] = acca

    s = _dot(qxb_ref[...].astype(BF16), slots(dfk))
    m_old = m_ref[...]
    m = jnp.maximum(m_old, jnp.max(s, axis=1, keepdims=True))
    alpha = jnp.exp(m_old - m)
    p = jnp.exp(s - m)
    l = alpha * l_ref[...] + jnp.sum(p, axis=1, keepdims=True)
    vb = jnp.concatenate(
        [jnp.concatenate([r[pl.ds(h, page, stride=DF_HEADS), :] for h in range(DF_HEADS)],
                         axis=1).astype(BF16) for r in dfv],
        axis=0)
    accb = alpha * accb_ref[...] + _dot(p.astype(BF16), vb)
    m_ref[...] = m
    l_ref[...] = l
    accb_ref[...] = accb

    @pl.when(c == pl.num_programs(1) - 1)
    def _():
        lane = lax.broadcasted_iota(jnp.int32, (n_new, width), 1)
        out = jnp.zeros((n_new, width), F32)
        for h in range(SB_HEADS):
            out = out + jnp.where(lane // SB_DIM == h, acca[h * n_new:(h + 1) * n_new, :], 0.0)
        oa_ref[...] = out.astype(oa_ref.dtype)
        o = accb / l
        for h in range(DF_HEADS):
            lanes = slice(h * LANES, (h + 1) * LANES)
            o0 = o[(2 * h) * n_new:(2 * h + 1) * n_new, lanes]
            o1 = o[(2 * h + 1) * n_new:(2 * h + 2) * n_new, lanes]
            oh = _head_norm(o0 - lam_ref[0] * o1, g_ref[h:h + 1, :], out_scale)
            ob_ref[:, lanes] = oh.astype(ob_ref.dtype)


def _decode_attention(layer, page_table, lam, qsb, ksbn, vsbn, qdf, kdfn, vdfn, g,
                      c_sbk, c_sbv, c_dfk, c_dfv, out_scale, pg=16):
    nb, n_new, width = qsb.shape
    n_pages = page_table.shape[1]
    page = c_sbk.shape[3]
    assert n_pages % pg == 0, (n_pages, pg)
    new_spec = pl.BlockSpec((None, n_new, width), lambda b, c, pt: (b, 0, 0))

    def page_specs(block):
        def spec(g_):
            def imap(b, c, pt):
                return (layer, pt[b, n_pages - 1 - (c * pg + g_)], 0, 0)
            return pl.BlockSpec((None, None) + block, imap)
        return [spec(g_) for g_ in range(pg)]

    t_specs = page_specs((width, page))
    v_specs = page_specs((page * DF_HEADS, LANES))
    nrow = SB_HEADS * n_new
    grid_spec = pltpu.PrefetchScalarGridSpec(
        num_scalar_prefetch=1,
        grid=(nb, n_pages // pg),
        in_specs=[pl.BlockSpec(memory_space=pltpu.SMEM)] + [new_spec] * 6
                 + [pl.BlockSpec(g.shape, lambda b, c, pt: (0, 0))] + t_specs * 3 + v_specs,
        out_specs=[new_spec, new_spec],
        scratch_shapes=[pltpu.VMEM((nrow, width), F32), pltpu.VMEM((nrow, width), F32),
                        pltpu.VMEM((nrow, 1), F32), pltpu.VMEM((nrow, width), F32),
                        pltpu.VMEM((nrow, 1), F32), pltpu.VMEM((nrow, 1), F32),
                        pltpu.VMEM((nrow, width), F32)],
    )
    return pl.pallas_call(
        functools.partial(_decode_kernel, pg=pg, n_new=n_new, out_scale=out_scale),
        grid_spec=grid_spec,
        out_shape=[jax.ShapeDtypeStruct((nb, n_new, width), BF16)] * 2,
        compiler_params=_params(("parallel", "arbitrary")),
        name="decode_attention",
    )(page_table, lam, qsb, ksbn, vsbn, qdf, kdfn, vdfn, g,
      *([c_sbk] * pg), *([c_sbv] * pg), *([c_dfk] * pg), *([c_dfv] * pg))


def _merge_kernel(x_ref, oa_ref, ob_ref, ga_ref, gb_ref, wa_ref, wb_ref, wo_ref, g_ref, b_ref,
                  h_ref, h16_ref, *, alpha):
    merged = (ga_ref[...].astype(F32) * _dot(oa_ref[...], wa_ref[...])
              + gb_ref[...].astype(F32) * _dot(ob_ref[...], wb_ref[...]))
    y = alpha * x_ref[...] + _dot(merged.astype(BF16), wo_ref[...])
    h = _layer_norm(y, g_ref[...], b_ref[...])
    h_ref[...] = h
    h16_ref[...] = h.astype(BF16)


def _merge(x, oa, ob, ga, gb, wa, wb, wo, g, b, alpha, tm):
    n, d = x.shape
    assert n % tm == 0, (n, tm)
    row = lambda i: (i, 0)
    const = lambda i: (0, 0)
    return pl.pallas_call(
        functools.partial(_merge_kernel, alpha=alpha),
        grid=(n // tm,),
        in_specs=[pl.BlockSpec((tm, d), row), pl.BlockSpec((tm, oa.shape[1]), row),
                  pl.BlockSpec((tm, ob.shape[1]), row), pl.BlockSpec((tm, d), row),
                  pl.BlockSpec((tm, d), row), pl.BlockSpec(wa.shape, const),
                  pl.BlockSpec(wb.shape, const), pl.BlockSpec(wo.shape, const),
                  pl.BlockSpec(g.shape, const), pl.BlockSpec(b.shape, const)],
        out_specs=[pl.BlockSpec((tm, d), row)] * 2,
        out_shape=[jax.ShapeDtypeStruct((n, d), F32), jax.ShapeDtypeStruct((n, d), BF16)],
        compiler_params=_params(("parallel",)),
        name="merge_ln",
    )(x, oa, ob, ga, gb, wa, wb, wo, g, b)


def _extract_topk(s, k, val_ref, idx_ref, row0):
    n = s.shape[0]
    iota = lax.broadcasted_iota(jnp.int32, s.shape, 0).astype(F32)
    for r in range(k):
        m = jnp.max(s, axis=0, keepdims=True)
        idx = jnp.min(jnp.where(s == m, iota, float(n)), axis=0, keepdims=True)
        val_ref[pl.ds(row0 + r, 1), :] = m
        idx_ref[pl.ds(row0 + r, 1), :] = idx
        s = jnp.where(iota == idx, -jnp.inf, s)


def _route_kernel(h_ref, wq_ref, keys_ref, i1_ref, i2_ref, g_ref,
                  s_ref, v_ref, i_ref, cand_ref, ci_ref, best_ref, pos_ref, sel_ref):
    tm = h_ref.shape[0]
    nsub = 2 * PEER_HEADS
    q = _dot(h_ref[...], wq_ref[...]).astype(BF16)
    for hc in range(nsub):
        s_ref[hc] = _dot_nt(keys_ref[hc], q[:, hc * PEER_HALF:(hc + 1) * PEER_HALF])

    def level1(hc, _):
        _extract_topk(s_ref[hc], PEER_TOPK, v_ref, i_ref, pl.multiple_of(hc * PEER_TOPK, PEER_TOPK))
        return 0

    lax.fori_loop(0, nsub, level1, 0)

    cand_ref[STAIR_ROWS:, :] = jnp.full((STAIR_PAD - STAIR_ROWS, tm), -jnp.inf, F32)
    ci_ref[STAIR_ROWS:, :] = jnp.zeros((STAIR_PAD - STAIR_ROWS, tm), F32)

    def level2(h, _):
        base1 = pl.multiple_of(2 * h * PEER_TOPK, PEER_TOPK)
        base2 = pl.multiple_of((2 * h + 1) * PEER_TOPK, PEER_TOPK)
        off = 0
        for r1, n2 in STAIR:
            cand_ref[off:off + n2, :] = v_ref[pl.ds(base1 + r1, 1), :] + v_ref[pl.ds(base2, n2), :]
            ci_ref[off:off + n2, :] = (i_ref[pl.ds(base1 + r1, 1), :] * PEER_NKEYS
                                       + i_ref[pl.ds(base2, n2), :])
            off += n2
        _extract_topk(cand_ref[...], PEER_TOPK, best_ref, pos_ref, 0)
        best = best_ref[...]
        e = jnp.exp(best - best[:1, :])
        gate = e / jnp.sum(e, axis=0, keepdims=True)
        ci = ci_ref[...]
        iota = lax.broadcasted_iota(jnp.int32, ci.shape, 0).astype(F32)
        out0 = pl.multiple_of(h * PEER_TOPK, PEER_TOPK)
        for r in range(PEER_TOPK):
            eidx = jnp.max(jnp.where(iota == pos_ref[r:r + 1, :], ci, -1.0), axis=0, keepdims=True)
            first = jnp.floor(eidx * (1.0 / PEER_NKEYS))
            sel_ref[0, pl.ds(out0 + r, 1), :] = first
            sel_ref[1, pl.ds(out0 + r, 1), :] = eidx - first * PEER_NKEYS
        sel_ref[2, pl.ds(out0, PEER_TOPK), :] = gate
        return 0

    lax.fori_loop(0, PEER_HEADS, level2, 0)
    for c0 in range(0, tm, LANES):
        i1_ref[c0:c0 + LANES, :] = sel_ref[0, :, c0:c0 + LANES].T
        i2_ref[c0:c0 + LANES, :] = sel_ref[1, :, c0:c0 + LANES].T
        g_ref[c0:c0 + LANES, :] = sel_ref[2, :, c0:c0 + LANES].T


def _route(h16, wq16, keys16, tm):
    n, d = h16.shape
    assert n % tm == 0 and tm % LANES == 0, (n, tm)
    nsub = 2 * PEER_HEADS
    nsel = PEER_HEADS * PEER_TOPK
    row = lambda i: (i, 0)
    return pl.pallas_call(
        _route_kernel,
        grid=(n // tm,),
        in_specs=[pl.BlockSpec((tm, d), row), pl.BlockSpec(wq16.shape, lambda i: (0, 0)),
                  pl.BlockSpec(keys16.shape, lambda i: (0, 0, 0))],
        out_specs=[pl.BlockSpec((tm, nsel), row)] * 3,
        out_shape=[jax.ShapeDtypeStruct((n, nsel), F32)] * 3,
        scratch_shapes=[pltpu.VMEM((nsub, PEER_NKEYS, tm), F32),
                        pltpu.VMEM((nsub * PEER_TOPK, tm), F32),
                        pltpu.VMEM((nsub * PEER_TOPK, tm), F32),
                        pltpu.VMEM((STAIR_PAD, tm), F32),
                        pltpu.VMEM((STAIR_PAD, tm), F32),
                        pltpu.VMEM((PEER_TOPK, tm), F32),
                        pltpu.VMEM((PEER_TOPK, tm), F32),
                        pltpu.VMEM((3, nsel, tm), F32)],
        compiler_params=_params(("parallel",)),
        name="peer_route",
    )(h16, wq16, keys16)


TOKENS_PER_STEP = 16
EXPERT_CHUNK = 4 * MXU_DIM


def _gelu(x):
    return 0.5 * x * (1.0 + lax.erf(x * (2.0 ** -0.5)))


def _experts_kernel(h_ref, h16_ref, i1_ref, i2_ref, gate_ref, u_ref, v_ref, lng_ref, lnb_ref,
                    o_ref, o16_ref, c_ref, *, alpha, stride):
    acc_ref = o_ref
    p, j = pl.program_id(1), pl.program_id(2)
    tm = h_ref.shape[0]
    te = v_ref.shape[0]
    keys = c_ref.shape[0] // stride

    @pl.when(jnp.logical_and(p == 0, j == 0))
    def _():
        acc_ref[...] = jnp.zeros_like(acc_ref)

    @pl.when(j == 0)
    def _():
        iota1 = (lax.broadcasted_iota(jnp.int32, (keys, PEER_NKEYS), 0) + p * keys).astype(F32)
        iota2 = lax.broadcasted_iota(jnp.int32, (PEER_NKEYS, PEER_NKEYS), 0).astype(F32)

        def tokens(i, _):
            t0 = i * TOKENS_PER_STEP
            pts, rts = [], []
            for u in range(TOKENS_PER_STEP):
                i1 = i1_ref[pl.ds(t0 + u, 1), :]
                i2 = i2_ref[pl.ds(t0 + u, 1), :]
                gt = gate_ref[pl.ds(t0 + u, 1), :]
                pts.append(jnp.where(iota1 == i1, 1.0, 0.0).astype(BF16))
                rts.append(jnp.where(iota2 == i2, gt, 0.0).astype(BF16))
            tiles = [_dot_nt(pt, rt) for pt, rt in zip(pts, rts)]
            for u, tile in enumerate(tiles):
                c_ref[pl.ds(t0 + u, keys, stride=stride), :] = tile
            return 0

        lax.fori_loop(0, tm // TOKENS_PER_STEP, tokens, 0)

    x = h16_ref[...]
    per = EXPERT_CHUNK // PEER_NKEYS
    y = None
    for r in range(te // EXPERT_CHUNK):
        rows = slice(r * EXPERT_CHUNK, (r + 1) * EXPERT_CHUNK)
        a = _gelu(_dot_nt(x, u_ref[rows, :]))
        first = (j * (te // EXPERT_CHUNK) + r) * per
        c = jnp.concatenate(
            [c_ref[pl.ds(pl.multiple_of((first + s) * stride, SUBLANES), tm), :] for s in range(per)],
            axis=1)
        part = _dot((a * c).astype(BF16), v_ref[rows, :])
        y = part if y is None else y + part
    acc_ref[...] += y

    @pl.when(jnp.logical_and(p == pl.num_programs(1) - 1, j == pl.num_programs(2) - 1))
    def _():
        out = _layer_norm(alpha * h_ref[...] + acc_ref[...], lng_ref[...], lnb_ref[...])
        o_ref[...] = out
        o16_ref[...] = out.astype(BF16)


def _experts(h, h16, i1, i2, gate, u16, v16, lng, lnb, alpha, tm, te, passes):
    n, d = h.shape
    assert n % tm == 0 and tm % TOKENS_PER_STEP == 0 and te % EXPERT_CHUNK == 0, (n, tm, te)
    assert v16.shape[0] % (te * passes) == 0 and PEER_NKEYS % passes == 0, (v16.shape, te, passes)
    ne = v16.shape[0]
    stride = tm + SUBLANES
    blocks = ne // (te * passes)
    row = lambda i, p, j: (i, 0)
    const = lambda i, p, j: (0, 0)
    blk = lambda i, p, j: (p * blocks + j, 0)
    nsel = i1.shape[1]
    once = functools.partial(pl.BlockSpec, index_map=row, pipeline_mode=pl.Buffered(1))
    return pl.pallas_call(
        functools.partial(_experts_kernel, alpha=alpha, stride=stride),
        grid=(n // tm, passes, blocks),
        in_specs=[once((tm, d)), once((tm, d)), once((tm, nsel)), once((tm, nsel)), once((tm, nsel)),
                  pl.BlockSpec((te, d), blk), pl.BlockSpec((te, d), blk),
                  pl.BlockSpec(lng.shape, const), pl.BlockSpec(lnb.shape, const)],
        out_specs=[pl.BlockSpec((tm, d), row)] * 2,
        out_shape=[jax.ShapeDtypeStruct((n, d), F32), jax.ShapeDtypeStruct((n, d), BF16)],
        scratch_shapes=[pltpu.VMEM((PEER_NKEYS // passes * stride, PEER_NKEYS), F32)],
        compiler_params=_params(("parallel", "arbitrary", "arbitrary")),
        name="peer_experts",
    )(h, h16, i1, i2, gate, u16, v16, lng, lnb)


def _rope_tables(pos):
    half = DF_DIM // 2
    inv = ROPE_THETA ** (-jnp.arange(half, dtype=F32) / half)
    ang = pos[:, None] * inv[None, :]
    cos, sin = jnp.cos(ang), jnp.sin(ang)
    reps = LANES // DF_DIM
    return (jnp.tile(jnp.concatenate([cos, cos], axis=1), (1, reps)),
            jnp.tile(jnp.concatenate([-sin, sin], axis=1), (1, reps)))


def _token_stage(x, x16, attn, lp, tm_proj, tm_route, exp_tiles, cos, sin, alpha, t_real):
    proj = _inproj(x16, lp["w_in"], lp["b_gate"], cos, sin, tm_proj, t_real)
    oa, ob = attn(proj)
    ga, gb = proj[10], proj[11]
    h, h16 = _merge(x, oa, ob, ga, gb, lp["w_br_a"], lp["w_br_b"], lp["w_o"], lp["ln1_g"], lp["ln1_b"],
                    alpha, tm_proj)
    i1, i2, gate = _route(h16, lp["peer_wq"], lp["peer_keys"], tm_route)
    y, y16 = _experts(h, h16, i1, i2, gate, lp["peer_u"], lp["peer_v"], lp["ln2_g"], lp["ln2_b"],
                      alpha, *exp_tiles)
    return y, y16, proj


def kernel(x_prompt, x_sample, cache_sb_k, cache_sb_v, cache_df_k, cache_df_v, page_table, meta_tokens, w_in, b_gate, w_br_a, w_br_b, w_o, lam_q1, lam_k1, lam_q2, lam_k2, df_norm_g, ln1_g, ln1_b, peer_wq, peer_sub_keys, peer_u, peer_v, ln2_g, ln2_b):
    depth, d = w_in.shape[0], w_in.shape[1]
    batch, seq, _ = x_prompt.shape
    nb, n_new, _ = x_sample.shape
    alpha = (2 * depth) ** 0.25
    t_real = seq + N_META
    t_pad = -(-t_real // Q_BLOCK) * Q_BLOCK
    n_pool, page = cache_sb_k.shape[1], cache_sb_k.shape[2]
    width = SB_HEADS * SB_DIM
    past_len = page_table.shape[1] * page

    meta = jnp.broadcast_to(meta_tokens[None].astype(F32), (batch, N_META, d))
    xp = jnp.concatenate([meta, x_prompt, jnp.zeros((batch, t_pad - t_real, d), F32)], axis=1)
    xp = xp.reshape(batch * t_pad, d)
    xp16 = xp.astype(BF16)
    xs = x_sample.reshape(nb * n_new, d)
    xs16 = xs.astype(BF16)
    cos_p, sin_p = _rope_tables(jnp.arange(t_pad, dtype=F32))
    cos_s, sin_s = _rope_tables(past_len + jnp.arange(n_new, dtype=F32))
    cos_s, sin_s = jnp.tile(cos_s, (nb, 1)), jnp.tile(sin_s, (nb, 1))
    caches = [jnp.moveaxis(c, 2, -1).reshape(depth, n_pool, width, page)
              for c in (cache_sb_k, cache_sb_v, cache_df_k)]
    caches.append(cache_df_v.reshape(depth, n_pool, page * DF_HEADS, 2 * DF_DIM))

    rows_p, rows_s = [], []
    for l in range(depth):
        lp = dict(
            w_in=w_in[l].astype(BF16), b_gate=b_gate[l].reshape(1, -1),
            w_br_a=w_br_a[l].astype(BF16), w_br_b=w_br_b[l].astype(BF16), w_o=w_o[l].astype(BF16),
            ln1_g=ln1_g[l].reshape(1, d), ln1_b=ln1_b[l].reshape(1, d),
            peer_wq=peer_wq[l].astype(BF16),
            peer_keys=peer_sub_keys[l].reshape(2 * PEER_HEADS, PEER_NKEYS, PEER_HALF).astype(BF16),
            peer_u=peer_u[l].astype(BF16), peer_v=peer_v[l].astype(BF16),
            ln2_g=ln2_g[l].reshape(1, d), ln2_b=ln2_b[l].reshape(1, d))
        lam_init = 0.8 - 0.6 * math.exp(-0.3 * l)
        lam = (jnp.exp(jnp.sum(lam_q1[l].astype(F32) * lam_k1[l].astype(F32)))
               - jnp.exp(jnp.sum(lam_q2[l].astype(F32) * lam_k2[l].astype(F32))) + lam_init).reshape(1)
        g_norm = df_norm_g[l].astype(F32)
        out_scale = 1.0 - lam_init

        def attn_prompt(proj):
            r3 = lambda a: a.reshape(batch, t_pad, width)
            oa = _sb_prefill(r3(proj[0]), r3(proj[6]), r3(proj[7]))
            ob = _df_prefill(lam, r3(proj[3]), r3(proj[8]), r3(proj[9]), g_norm, out_scale)
            return oa.reshape(-1, width), ob.reshape(-1, width)

        def attn_sample(proj):
            r3 = lambda a: a.reshape(nb, n_new, width)
            oa, ob = _decode_attention(l, page_table, lam, r3(proj[0]), r3(proj[1]), r3(proj[2]),
                                       r3(proj[3]), r3(proj[4]), r3(proj[5]), g_norm,
                                       *caches, out_scale)
            return oa.reshape(-1, width), ob.reshape(-1, width)

        xp, xp16, proj_p = _token_stage(xp, xp16, attn_prompt, lp, t_pad // 4, 1024, (512, 1024, 1),
                                        cos_p, sin_p, alpha, t_real)
        xs, xs16, proj_s = _token_stage(xs, xs16, attn_sample, lp, nb * n_new, 256, (256, 2048, 1),
                                        cos_s, sin_s, alpha, nb * n_new)
        rows_p.append([proj_p[i] for i in (1, 2, 4, 5)])
        rows_s.append([proj_s[i].reshape(nb, n_new, width) for i in (1, 2, 4, 5)])

    y_prompt = xp.reshape(batch, t_pad, d)[:, N_META:t_real]
    y_sample = xs.reshape(nb, n_new, d)

    def stack(rows, i, shape):
        return jnp.stack([r[i] for r in rows]).reshape(shape)

    tp, ts = (depth, batch, t_real), (depth, nb, n_new)
    return (y_prompt, y_sample,
            stack(rows_p, 0, tp + (SB_HEADS, SB_DIM)), stack(rows_p, 1, tp + (SB_HEADS, SB_DIM)),
            stack(rows_p, 2, tp + (DF_HEADS, 2, DF_DIM)), stack(rows_p, 3, tp + (DF_HEADS, 2 * DF_DIM)),
            stack(rows_s, 0, ts + (SB_HEADS, SB_DIM)), stack(rows_s, 1, ts + (SB_HEADS, SB_DIM)),
            stack(rows_s, 2, ts + (DF_HEADS, 2, DF_DIM)), stack(rows_s, 3, ts + (DF_HEADS, 2 * DF_DIM)))
```
